```python
import jax, jax.numpy as jnp
from jax import lax
import numpy as np

D_MODEL = 4096
BATCH = 1
SEQ = 8192
DEPTH = 2
DEC_BATCH = 16
DEC_SEQ = 32
PAST_LEN = 2048

CHUNK = 64
N_BRANCH = 4
BRANCH_W = D_MODEL // 4
EPS = 1e-6
F32 = jnp.float32

GLA_HEADS = 4
GLA_DK = BRANCH_W // (2 * GLA_HEADS)
GLA_DV = BRANCH_W // GLA_HEADS
GLA_QK = GLA_HEADS * GLA_DK
GLA_RANK = 16
GLA_NORMALIZER = 16.0
GLA_BLOCK = 16

RW_DH = 64
RW_HEADS = BRANCH_W // RW_DH
RW_W_RANK = 64
RW_A_RANK = 64
RW_G_RANK = 128
RW_SPLIT = (BRANCH_W, BRANCH_W, BRANCH_W, RW_W_RANK, RW_A_RANK, RW_G_RANK)
RW_NCOLS = 3 * BRANCH_W + RW_W_RANK + RW_A_RANK + RW_G_RANK
RW_DECAY_SCALE = 0.606531
RW_LN_EPS = 64e-5

ML_HEADS = 4
ML_DH = BRANCH_W // ML_HEADS
ML_BLOCK = CHUNK
CONV_W = 4

LRU_BLOCKS = 8
LRU_BW = BRANCH_W // LRU_BLOCKS
LRU_C = 8.0

N_GROUPS = 4
EXP_PER_GROUP = 4
N_EXPERTS = N_GROUPS * EXP_PER_GROUP
TOP_K = 2
D_FF_EXPERT = D_MODEL // 4

MERGE_RANK = 256

IN_SPLIT = (GLA_QK, GLA_QK, BRANCH_W, BRANCH_W, GLA_RANK,
            RW_NCOLS,
            2 * BRANCH_W, BRANCH_W, BRANCH_W, 2 * ML_HEADS,
            BRANCH_W, BRANCH_W)
N_IN = 2 * GLA_QK + 2 * BRANCH_W + GLA_RANK + RW_NCOLS + 4 * BRANCH_W + 2 * ML_HEADS + 2 * BRANCH_W

kernel_name = 'hybrid_stream_encoder_step'


def split_last(x, sizes):
    return jnp.split(x, np.cumsum(sizes)[:-1].tolist(), axis=-1)


def rms_norm(x, g):
    xf = x.astype(F32)
    return (xf * lax.rsqrt(jnp.mean(xf * xf, -1, keepdims=True) + EPS) * g).astype(x.dtype)


def head_rms_norm(x, g):
    return x * lax.rsqrt(jnp.mean(x * x, -1, keepdims=True) + EPS) * g


def causal_conv(u, buf, w, b):
    T = u.shape[1]
    full = jnp.concatenate([buf.astype(u.dtype), u], axis=1)
    out = b + full[:, 0:T] * w[0]
    for j in range(1, CONV_W):
        out = out + full[:, j:j + T] * w[j]
    return out, full[:, T:]


def to_blocks(u, L, fill=0.0):
    B, T = u.shape[:2]
    n = -(-T // L)
    pad = [(0, 0)] * u.ndim
    pad[1] = (0, n * L - T)
    u = jnp.pad(u, pad, constant_values=fill)
    u = u.reshape((B, n, L) + u.shape[2:])
    u = jnp.moveaxis(u, 1, 0)
    return jnp.swapaxes(u, 2, 3)


def from_blocks(o, T):
    n, B, H, L = o.shape[:4]
    o = jnp.swapaxes(o, 2, 3)
    o = jnp.moveaxis(o, 0, 1).reshape((B, n * L) + o.shape[3:])
    return o[:, :T]


def gla_mixer(q, k, v, g, a_low, S0, w_up, b_a, g_norm):
    B, T, _ = q.shape
    heads = lambda u, d: u.astype(F32).reshape(B, T, GLA_HEADS, d)
    qh = heads(q, GLA_DK) * (GLA_DK ** -0.5)
    kh = heads(k, GLA_DK)
    vh = heads(v, GLA_DV)
    log_a = heads(jax.nn.log_sigmoid((a_low @ w_up).astype(F32) + b_a) / GLA_NORMALIZER, GLA_DK)
    L = min(GLA_BLOCK, T)
    causal = jnp.tril(jnp.ones((L, L), bool))[:, :, None]

    def step(S, inp):
        qc, kc, vc, lc = inp
        bc = jnp.cumsum(lc, axis=2)
        wts = jnp.exp(jnp.where(causal, bc[:, :, :, None, :] - bc[:, :, None, :, :], -jnp.inf))
        att = jnp.einsum('bhtd,bhsd,bhtsd->bhts', qc, kc, wts)
        o = jnp.einsum('bhts,bhsv->bhtv', att, vc) + jnp.einsum('bhtd,bhdv->bhtv', qc * jnp.exp(bc), S)
        b_last = bc[:, :, -1:, :]
        S = S * jnp.exp(b_last[:, :, 0, :, None]) + jnp.einsum('bhsd,bhsv->bhdv', kc * jnp.exp(b_last - bc), vc)
        return S, o

    S, o = lax.scan(step, S0.astype(F32),
                    (to_blocks(qh, L), to_blocks(kh, L), to_blocks(vh, L), to_blocks(log_a, L)))
    o = head_rms_norm(from_blocks(o, T), g_norm).reshape(B, T, BRANCH_W)
    return (o * jax.nn.silu(g.astype(F32))).astype(q.dtype), S.astype(S0.dtype)


def rwkv7_mixer(p, shift_prev, S0, mu, w0, w_up, a0, a_up, g_up, k_k, k_a, r_k, ln_g, ln_b):
    B, T, _ = p.shape
    prev = jnp.concatenate([shift_prev[:, None].astype(p.dtype), p[:, :-1]], axis=1)
    pm = (p + (prev - p) * mu).astype(F32)
    r, k, v, wd, ad, gd = split_last(pm, RW_SPLIT)
    log_w = -RW_DECAY_SCALE * jax.nn.sigmoid(w0 + jnp.tanh(wd) @ w_up)
    a = jax.nn.sigmoid(a0 + ad @ a_up)
    g = jax.nn.sigmoid(gd) @ g_up
    heads = lambda u: u.reshape(B, T, RW_HEADS, RW_DH)
    kk = heads(k * k_k)
    kk = kk / jnp.maximum(jnp.sqrt(jnp.sum(kk * kk, -1, keepdims=True)), 1e-12)
    k = k * (1.0 + (a - 1.0) * k_a)
    r, k, v, a, log_w = heads(r), heads(k), heads(v), heads(a), heads(log_w)

    def step(S, inp):
        rt, kt, vt, at, wt, kkt = inp
        sa = jnp.einsum('bhvk,bhk->bhv', S, -kkt)
        S = (S * jnp.exp(wt)[:, :, None, :] + sa[..., None] * (kkt * at)[:, :, None, :]
             + vt[..., None] * kt[:, :, None, :])
        return S, jnp.einsum('bhvk,bhk->bhv', S, rt)

    tm = lambda u: jnp.moveaxis(u, 1, 0)
    S, y = lax.scan(step, S0.astype(F32), (tm(r), tm(k), tm(v), tm(a), tm(log_w), tm(kk)))
    y = jnp.moveaxis(y, 0, 1)
    mean = jnp.mean(y, -1, keepdims=True)
    var = jnp.mean(jnp.square(y - mean), -1, keepdims=True)
    y = (y - mean) * lax.rsqrt(var + RW_LN_EPS) * ln_g.reshape(RW_HEADS, RW_DH) + ln_b.reshape(RW_HEADS, RW_DH)
    y = y + jnp.sum(r * k * r_k, -1, keepdims=True) * v
    out = y.reshape(B, T, BRANCH_W) * g
    return out.astype(p.dtype), p[:, -1].astype(shift_prev.dtype), S.astype(S0.dtype)


def mlstm_mixer(p_qk, p_v, p_o, p_if, conv_buf, C0, n0, m0, conv_w, conv_b, b_if, norm_g):
    B, T, _ = p_v.shape
    qk, new_buf = causal_conv(p_qk, conv_buf, conv_w, conv_b)
    q, k = jnp.split(jax.nn.silu(qk.astype(F32)), 2, axis=-1)
    heads = lambda u: u.reshape(B, T, ML_HEADS, ML_DH)
    q, k, v = heads(q), heads(k) * (ML_DH ** -0.5), heads(p_v.astype(F32))
    gates = p_if.astype(F32) + b_if
    i_pre = gates[..., :ML_HEADS]
    log_f = jax.nn.log_sigmoid(gates[..., ML_HEADS:])
    L = min(ML_BLOCK, T)
    causal = jnp.tril(jnp.ones((L, L), bool))

    def step(carry, inp):
        C, n, m = carry
        qc, kc, vc, ic, lfc = inp
        F = jnp.cumsum(lfc, axis=-1)
        log_d = jnp.where(causal, F[..., :, None] - F[..., None, :] + ic[..., None, :], -jnp.inf)
        log_inter = F + m[..., None]
        m_t = jnp.maximum(log_inter, jnp.max(log_d, axis=-1))
        s = jnp.einsum('bhtd,bhsd->bhts', qc, kc) * jnp.exp(log_d - m_t[..., None])
        inter = jnp.exp(log_inter - m_t)
        num = jnp.einsum('bhts,bhsv->bhtv', s, vc) + inter[..., None] * jnp.einsum('bhtk,bhkv->bhtv', qc, C)
        den = jnp.sum(s, -1) + inter * jnp.einsum('bhtk,bhk->bht', qc, n)
        h = num / jnp.maximum(jnp.abs(den), jnp.exp(-m_t))[..., None]
        m_new = m_t[..., -1]
        decay = jnp.exp(F[..., -1] + m - m_new)
        w_s = jnp.exp(F[..., -1:] - F + ic - m_new[..., None])
        C = decay[..., None, None] * C + jnp.einsum('bhs,bhsk,bhsv->bhkv', w_s, kc, vc)
        n = decay[..., None] * n + jnp.einsum('bhs,bhsk->bhk', w_s, kc)
        return (C, n, m_new), h

    (C, n, m), h = lax.scan(step, (C0.astype(F32), n0.astype(F32), m0.astype(F32)),
                            (to_blocks(q, L), to_blocks(k, L), to_blocks(v, L),
                             to_blocks(i_pre, L, -jnp.inf), to_blocks(log_f, L)))
    h = head_rms_norm(from_blocks(h, T), norm_g.reshape(ML_HEADS, ML_DH)).reshape(B, T, BRANCH_W)
    y = h * jax.nn.sigmoid(p_o.astype(F32))
    return y.astype(p_v.dtype), new_buf, C.astype(C0.dtype), n.astype(n0.dtype), m.astype(m0.dtype)


def rglru_mixer(p_x, p_gate, conv_buf, h0, conv_w, conv_b, wa, ba, wx, bx, lam):
    B, T, _ = p_x.shape
    xc, new_buf = causal_conv(p_x, conv_buf, conv_w, conv_b)
    xf = xc.astype(F32)
    xb = xf.reshape(B, T, LRU_BLOCKS, LRU_BW)
    r = jax.nn.sigmoid(jnp.einsum('btnc,ncd->btnd', xb, wa).reshape(B, T, BRANCH_W) + ba)
    i = jax.nn.sigmoid(jnp.einsum('btnc,ncd->btnd', xb, wx).reshape(B, T, BRANCH_W) + bx)
    log_a = LRU_C * r * jax.nn.log_sigmoid(lam.astype(F32))
    a = jnp.exp(log_a)
    u = jnp.sqrt(-jnp.expm1(2.0 * log_a)) * (i * xf)

    def combine(e1, e2):
        return e1[0] * e2[0], e2[0] * e1[1] + e2[1]

    a_cum, u_cum = lax.associative_scan(combine, (a, u), axis=1)
    h = a_cum * h0.astype(F32)[:, None] + u_cum
    y = jax.nn.gelu(p_gate.astype(F32)) * h
    return y.astype(p_x.dtype), new_buf, h[:, -1].astype(h0.dtype)


def hier_moe(h, w_route_g, b_route_g, w_route_e, b_route_e, w_e_gate, w_e_up, w_e_down):
    B, T, D = h.shape
    hf = h.reshape(B * T, D)
    g_logits = (hf @ w_route_g).astype(F32) + b_route_g
    g_top, grp = lax.top_k(g_logits, 1)
    p_grp = jnp.exp(g_top - jax.nn.logsumexp(g_logits, axis=-1, keepdims=True))
    e_logits = ((hf @ w_route_e).astype(F32) + b_route_e).reshape(-1, N_GROUPS, EXP_PER_GROUP)
    e_in = jnp.einsum('ng,nge->ne', jax.nn.one_hot(grp[:, 0], N_GROUPS, dtype=F32), e_logits)
    e_top, e_idx = lax.top_k(e_in, TOP_K)
    wts = jax.nn.softmax(e_top, axis=-1) * p_grp
    eid = grp * EXP_PER_GROUP + e_idx
    combine = jnp.einsum('nk,nke->ne', wts, jax.nn.one_hot(eid, N_EXPERTS, dtype=F32)).astype(h.dtype)
    y = jnp.zeros_like(hf)
    for e in range(N_EXPERTS):
        hid = jax.nn.silu(hf @ w_e_gate[e]) * (hf @ w_e_up[e])
        y = y + combine[:, e:e + 1] * (hid @ w_e_down[e])
    return y.reshape(B, T, D)


def trunk_layer(x, c, st, w_ada, b_ada, g_norm1, g_norm2, w_in, gla_w_up, gla_b, gla_g_norm,
                rw_mu, rw_w0, rw_w_up, rw_a0, rw_a_up, rw_g_up, rw_k_k, rw_k_a, rw_r_k, rw_ln_g, rw_ln_b,
                ml_conv_w, ml_conv_b, ml_b_if, ml_norm_g,
                lru_conv_w, lru_conv_b, lru_wa, lru_ba, lru_wx, lru_bx, lru_lambda,
                w_branch, w_mg_down, w_mg_up, b_mg, w_out,
                w_route_g, b_route_g, w_route_e, b_route_e, w_e_gate, w_e_up, w_e_down):
    gla_S, rw_S, rw_shift, ml_C, ml_n, ml_m, ml_conv, lru_h, lru_conv = st
    B, T, D = x.shape
    mod = (jax.nn.silu(c) @ w_ada + b_ada)[:, None, :]
    shift1, scale1, gate1, shift2, scale2, gate2 = jnp.split(mod, 6, axis=-1)
    h = rms_norm(x, g_norm1) * (1.0 + scale1) + shift1
    gq, gk, gv, gg, ga, prw, mqk, mv, mo, mif, lx, lg = split_last(h @ w_in, IN_SPLIT)
    y_a, gla_S = gla_mixer(gq, gk, gv, gg, ga, gla_S, gla_w_up, gla_b, gla_g_norm)
    y_b, rw_shift, rw_S = rwkv7_mixer(prw, rw_shift, rw_S, rw_mu, rw_w0, rw_w_up, rw_a0, rw_a_up,
                                      rw_g_up, rw_k_k, rw_k_a, rw_r_k, rw_ln_g, rw_ln_b)
    y_c, ml_conv, ml_C, ml_n, ml_m = mlstm_mixer(mqk, mv, mo, mif, ml_conv, ml_C, ml_n, ml_m,
                                                 ml_conv_w, ml_conv_b, ml_b_if, ml_norm_g)
    y_d, lru_conv, lru_h = rglru_mixer(lx, lg, lru_conv, lru_h, lru_conv_w, lru_conv_b,
                                       lru_wa, lru_ba, lru_wx, lru_bx, lru_lambda)
    gates = jax.nn.sigmoid(((h @ w_mg_down) @ w_mg_up + b_mg).astype(F32)).astype(x.dtype)
    gates = gates.reshape(B, T, N_BRANCH, D)
    merged = (gates[:, :, 0] * (y_a @ w_branch[0]) + gates[:, :, 1] * (y_b @ w_branch[1])
              + gates[:, :, 2] * (y_c @ w_branch[2]) + gates[:, :, 3] * (y_d @ w_branch[3]))
    x = x + gate1 * (merged @ w_out)
    h2 = rms_norm(x, g_norm2) * (1.0 + scale2) + shift2
    x = x + gate2 * hier_moe(h2, w_route_g, b_route_g, w_route_e, b_route_e, w_e_gate, w_e_up, w_e_down)
    return x, (gla_S, rw_S, rw_shift, ml_C, ml_n, ml_m, ml_conv, lru_h, lru_conv)


def zero_states(batch, dtype):
    z = lambda *s: jnp.zeros((DEPTH, batch) + s, dtype)
    return (z(GLA_HEADS, GLA_DK, GLA_DV), z(RW_HEADS, RW_DH, RW_DH), z(RW_NCOLS),
            z(ML_HEADS, ML_DH, ML_DH), z(ML_HEADS, ML_DH), z(ML_HEADS),
            z(CONV_W - 1, 2 * BRANCH_W), z(BRANCH_W), z(CONV_W - 1, BRANCH_W))


def run_trunk(x, c, states, layer_weights, g_final):
    new = []
    for l in range(DEPTH):
        x, st = trunk_layer(x, c, tuple(s[l] for s in states), *[w[l] for w in layer_weights])
        new.append(st)
    return rms_norm(x, g_final), [jnp.stack([st[i] for st in new]) for i in range(len(states))]


def setup_inputs(seed: int = 0) -> dict:
    key = jax.random.key(seed)
    keys = iter(jax.random.split(key, 96))

    def nrm(shape, scale):
        return jax.random.normal(next(keys), shape, F32) * scale

    def gain(shape):
        return 1.0 + nrm(shape, 0.02)

    D, W, L, E, FF = D_MODEL, BRANCH_W, DEPTH, N_EXPERTS, D_FF_EXPERT
    lam_u = jax.random.uniform(next(keys), (L, W), F32, 0.9, 0.999)
    lam_s = lam_u ** (1.0 / LRU_C)
    rw_mu = jax.random.uniform(next(keys), (L, RW_NCOLS), F32)
    if_offset = jnp.concatenate([jnp.zeros((ML_HEADS,), F32), jnp.full((ML_HEADS,), 3.0, F32)])
    return {
        'x_prompt': nrm((BATCH, SEQ, D), 1.0),
        'x_sample': nrm((DEC_BATCH, DEC_SEQ, D), 1.0),
        'c_prompt': nrm((BATCH, D), 1.0),
        'c_sample': nrm((DEC_BATCH, D), 1.0),
        'state_gla_S': nrm((L, DEC_BATCH, GLA_HEADS, GLA_DK, GLA_DV), 0.5),
        'state_rwkv_S': nrm((L, DEC_BATCH, RW_HEADS, RW_DH, RW_DH), 0.3),
        'state_rwkv_shift': nrm((L, DEC_BATCH, RW_NCOLS), 1.0),
        'state_mlstm_C': nrm((L, DEC_BATCH, ML_HEADS, ML_DH, ML_DH), 0.3),
        'state_mlstm_n': nrm((L, DEC_BATCH, ML_HEADS, ML_DH), 0.3),
        'state_mlstm_m': nrm((L, DEC_BATCH, ML_HEADS), 1.0),
        'state_mlstm_conv': nrm((L, DEC_BATCH, CONV_W - 1, 2 * W), 1.0),
        'state_lru_h': nrm((L, DEC_BATCH, W), 0.5),
        'state_lru_conv': nrm((L, DEC_BATCH, CONV_W - 1, W), 1.0),
        'w_ada': nrm((L, D, 6 * D), 0.5 * D ** -0.5),
        'b_ada': nrm((L, 6 * D), 0.02),
        'g_norm1': gain((L, D)),
        'g_norm2': gain((L, D)),
        'w_in': nrm((L, D, N_IN), D ** -0.5),
        'gla_w_up': nrm((L, GLA_RANK, GLA_QK), GLA_RANK ** -0.5),
        'gla_b': 1.0 + nrm((L, GLA_QK), 0.1),
        'gla_g_norm': gain((L, GLA_DV)),
        'rw_mu': rw_mu,
        'rw_w0': nrm((L, W), 0.5),
        'rw_w_up': nrm((L, RW_W_RANK, W), 0.5 * RW_W_RANK ** -0.5),
        'rw_a0': nrm((L, W), 0.1),
        'rw_a_up': nrm((L, RW_A_RANK, W), 0.5 * RW_A_RANK ** -0.5),
        'rw_g_up': nrm((L, RW_G_RANK, W), RW_G_RANK ** -0.5),
        'rw_k_k': 0.85 + nrm((L, W), 0.05),
        'rw_k_a': 1.0 + nrm((L, W), 0.05),
        'rw_r_k': nrm((L, RW_HEADS, RW_DH), 0.1),
        'rw_ln_g': gain((L, W)),
        'rw_ln_b': nrm((L, W), 0.02),
        'ml_conv_w': nrm((L, CONV_W, 2 * W), 0.5),
        'ml_conv_b': nrm((L, 2 * W), 0.02),
        'ml_b_if': nrm((L, 2 * ML_HEADS), 0.1) + if_offset,
        'ml_norm_g': gain((L, W)),
        'lru_conv_w': nrm((L, CONV_W, W), 0.5),
        'lru_conv_b': nrm((L, W), 0.02),
        'lru_wa': nrm((L, LRU_BLOCKS, LRU_BW, LRU_BW), LRU_BW ** -0.5),
        'lru_ba': nrm((L, W), 0.02),
        'lru_wx': nrm((L, LRU_BLOCKS, LRU_BW, LRU_BW), LRU_BW ** -0.5),
        'lru_bx': nrm((L, W), 0.02),
        'lru_lambda': jnp.log(lam_s) - jnp.log1p(-lam_s),
        'w_branch': nrm((L, N_BRANCH, W, D), W ** -0.5),
        'w_mg_down': nrm((L, D, MERGE_RANK), D ** -0.5),
        'w_mg_up': nrm((L, MERGE_RANK, N_BRANCH * D), MERGE_RANK ** -0.5),
        'b_mg': nrm((L, N_BRANCH * D), 0.02),
        'w_out': nrm((L, D, D), D ** -0.5),
        'w_route_g': nrm((L, D, N_GROUPS), D ** -0.5),
        'b_route_g': nrm((L, N_GROUPS), 0.01),
        'w_route_e': nrm((L, D, E), D ** -0.5),
        'b_route_e': nrm((L, E), 0.01),
        'w_e_gate': nrm((L, E, D, FF), D ** -0.5),
        'w_e_up': nrm((L, E, D, FF), D ** -0.5),
        'w_e_down': nrm((L, E, FF, D), FF ** -0.5),
        'g_final': gain((D,)),
    }


def reference(x_prompt, x_sample, c_prompt, c_sample,
              state_gla_S, state_rwkv_S, state_rwkv_shift, state_mlstm_C, state_mlstm_n,
              state_mlstm_m, state_mlstm_conv, state_lru_h, state_lru_conv,
              w_ada, b_ada, g_norm1, g_norm2, w_in, gla_w_up, gla_b, gla_g_norm,
              rw_mu, rw_w0, rw_w_up, rw_a0, rw_a_up, rw_g_up, rw_k_k, rw_k_a, rw_r_k, rw_ln_g, rw_ln_b,
              ml_conv_w, ml_conv_b, ml_b_if, ml_norm_g,
              lru_conv_w, lru_conv_b, lru_wa, lru_ba, lru_wx, lru_bx, lru_lambda,
              w_branch, w_mg_down, w_mg_up, b_mg, w_out,
              w_route_g, b_route_g, w_route_e, b_route_e, w_e_gate, w_e_up, w_e_down,
              g_final):
    layer_weights = (w_ada, b_ada, g_norm1, g_norm2, w_in, gla_w_up, gla_b, gla_g_norm,
                     rw_mu, rw_w0, rw_w_up, rw_a0, rw_a_up, rw_g_up, rw_k_k, rw_k_a, rw_r_k, rw_ln_g, rw_ln_b,
                     ml_conv_w, ml_conv_b, ml_b_if, ml_norm_g,
                     lru_conv_w, lru_conv_b, lru_wa, lru_ba, lru_wx, lru_bx, lru_lambda,
                     w_branch, w_mg_down, w_mg_up, b_mg, w_out,
                     w_route_g, b_route_g, w_route_e, b_route_e, w_e_gate, w_e_up, w_e_down)
    y_prompt, ps = run_trunk(x_prompt, c_prompt, zero_states(x_prompt.shape[0], x_prompt.dtype),
                             layer_weights, g_final)
    sample_states = (state_gla_S, state_rwkv_S, state_rwkv_shift, state_mlstm_C, state_mlstm_n,
                     state_mlstm_m, state_mlstm_conv, state_lru_h, state_lru_conv)
    y_sample, ss = run_trunk(x_sample, c_sample, sample_states, layer_weights, g_final)
    p_gla_S, p_rwkv_S, p_rwkv_shift, p_mlstm_C, p_mlstm_n, p_mlstm_m, p_mlstm_conv, p_lru_h, p_lru_conv = ps
    s_gla_S, s_rwkv_S, s_rwkv_shift, s_mlstm_C, s_mlstm_n, s_mlstm_m, s_mlstm_conv, s_lru_h, s_lru_conv = ss
    return (y_prompt, y_sample,
            p_gla_S, p_rwkv_S, p_rwkv_shift, p_mlstm_C, p_mlstm_n, p_mlstm_m, p_mlstm_conv, p_lru_h, p_lru_conv,
            s_gla_S, s_rwkv_S, s_rwkv_shift, s_mlstm_C, s_mlstm_n, s_mlstm_m, s_mlstm_conv, s_lru_h, s_lru_conv)
```

```python
import functools

import jax
import jax.numpy as jnp
from jax import lax
from jax.experimental import pallas as pl
from jax.experimental.pallas import tpu as pltpu

F32 = jnp.float32
BF16 = jnp.bfloat16

D_MODEL = 4096
BRANCH_W = D_MODEL // 4
EPS = 1e-6
GLA_HEADS = 4
GLA_DK = 128
GLA_DV = 256
GLA_QK = GLA_HEADS * GLA_DK
GLA_RANK = 16
GLA_NORMALIZER = 16.0
RW_DH = 64
RW_HEADS = BRANCH_W // RW_DH
RW_NCOLS = 3 * BRANCH_W + 256
RW_DECAY_SCALE = 0.606531
RW_LN_EPS = 64e-5
ML_HEADS = 4
ML_DH = BRANCH_W // ML_HEADS
CONV_W = 4
LRU_BLOCKS = 8
LRU_BW = BRANCH_W // LRU_BLOCKS
LRU_C = 8.0
N_GROUPS = 4
EXP_PER_GROUP = 4
N_EXPERTS = 16
D_FF = D_MODEL // 4
MERGE_RANK = 256

VMEM_LIMIT = 56 * 1024 * 1024
CHUNK = 128
RW_SUB = 32
NEG_BIG = -1e30

NT_DIMS = (((1,), (1,)), ((), ()))
TN_DIMS = (((0,), (0,)), ((), ()))


def _cparams(sem):
    return pltpu.CompilerParams(dimension_semantics=sem, vmem_limit_bytes=VMEM_LIMIT)


def _dot(a, b):
    return jnp.dot(a.astype(BF16), b.astype(BF16), preferred_element_type=F32)


def _dot_nt(a, b):
    return lax.dot_general(a.astype(BF16), b.astype(BF16), NT_DIMS, preferred_element_type=F32)


def _dot_tn(a, b):
    return lax.dot_general(a.astype(BF16), b.astype(BF16), TN_DIMS, preferred_element_type=F32)


def _split3(x):
    hi = x.astype(BF16)
    r1 = x - hi.astype(F32)
    mid = r1.astype(BF16)
    lo = (r1 - mid.astype(F32)).astype(BF16)
    return hi, mid, lo


def _split2(x):
    hi = x.astype(BF16)
    lo = (x - hi.astype(F32)).astype(BF16)
    return hi, lo


def _dot_x3(a, b_exact):
    hi, mid, lo = _split3(a)
    f = lambda u: jnp.dot(u, b_exact, preferred_element_type=F32)
    return f(hi) + f(mid) + f(lo)


def _dot_3x(a_exact, b):
    hi, mid, lo = _split3(b)
    f = lambda u: jnp.dot(a_exact, u, preferred_element_type=F32)
    return f(hi) + f(mid) + f(lo)


def _dot_hl(a, b):
    ah, al = _split2(a)
    bh, bl = _split2(b)
    f = lambda u, v: jnp.dot(u, v, preferred_element_type=F32)
    return f(ah, bh) + f(al, bh) + f(ah, bl)


def _dot_hl_nt(a, b):
    ah, al = _split2(a)
    bh, bl = _split2(b)
    f = lambda u, v: lax.dot_general(u, v, NT_DIMS, preferred_element_type=F32)
    return f(ah, bh) + f(al, bh) + f(ah, bl)


def _sigmoid(x):
    return 1.0 / (1.0 + jnp.exp(-x))


def _silu(x):
    return x * _sigmoid(x)


def _log_sigmoid(x):
    return jnp.minimum(x, 0.0) - jnp.log(1.0 + jnp.exp(-jnp.abs(x)))


def _gelu_tanh(x):
    return 0.5 * x * (1.0 + jnp.tanh(0.7978845608028654 * (x + 0.044715 * x * x * x)))


def _tri(n, upper=False):
    r = lax.broadcasted_iota(jnp.int32, (n, n), 0)
    c = lax.broadcasted_iota(jnp.int32, (n, n), 1)
    m = (r <= c) if upper else (r >= c)
    return m


def _seg_ones(width, seg):
    r = lax.broadcasted_iota(jnp.int32, (width, width), 0) // seg
    c = lax.broadcasted_iota(jnp.int32, (width, width), 1) // seg
    return jnp.where(r == c, 1.0, 0.0).astype(BF16)


def _segsum_bf16(x_bf16, ones_blk):
    bw = ones_blk.shape[0]
    n = x_bf16.shape[1] // bw
    return jnp.concatenate(
        [jnp.dot(x_bf16[:, g * bw:(g + 1) * bw], ones_blk, preferred_element_type=F32) for g in range(n)],
        axis=1)


def _segsum3(x, ones_blk):
    hi, mid, lo = _split3(x)
    return _segsum_bf16(hi, ones_blk) + _segsum_bf16(mid, ones_blk) + _segsum_bf16(lo, ones_blk)


def _pad_rows(x, rows):
    if x.shape[0] == rows:
        return x
    return jnp.concatenate([x, jnp.zeros((rows - x.shape[0],) + x.shape[1:], x.dtype)], axis=0)


def _ada_kernel(c_ref, w_ref, b_ref, o_ref):
    c = c_ref[...]
    o_ref[...] = _dot(_silu(c), w_ref[...]) + b_ref[...]


def _ada_mod(c_all, w_ada, b_ada, tn=512):
    n_layers, d, n = w_ada.shape
    rows = c_all.shape[0]
    return pl.pallas_call(
        _ada_kernel,
        grid=(n_layers, n // tn),
        in_specs=[pl.BlockSpec((rows, d), lambda l, j: (0, 0)),
                  pl.BlockSpec((None, d, tn), lambda l, j: (l, 0, j)),
                  pl.BlockSpec((None, 1, tn), lambda l, j: (l, 0, j))],
        out_specs=pl.BlockSpec((None, rows, tn), lambda l, j: (l, 0, j)),
        out_shape=jax.ShapeDtypeStruct((n_layers, rows, n), F32),
        compiler_params=_cparams(("arbitrary", "arbitrary")),
        name="ada_mod",
    )(c_all, w_ada, b_ada.reshape(n_layers, 1, n))


def _row_blocks(B, T, max_rows):
    if B == 1:
        return 1, min(T, max_rows)
    bb = max(1, min(B, max_rows // T))
    assert B % bb == 0
    return bb, T


def _modnorm(x, g, scale, shift):
    ms = jnp.mean(x * x, axis=-1, keepdims=True)
    h = x * lax.rsqrt(ms + EPS) * g
    return h * (1.0 + scale) + shift


def _modnorm_kernel(x_ref, g_ref, sc_ref, sh_ref, o_ref):
    o_ref[...] = _modnorm(x_ref[...], g_ref[...], sc_ref[...], sh_ref[...]).astype(o_ref.dtype)


def _modnorm_call(x, g, scale, shift):
    B, T, D = x.shape
    bb, tt = _row_blocks(B, T, 512)
    return pl.pallas_call(
        _modnorm_kernel,
        grid=(B // bb, T // tt),
        in_specs=[pl.BlockSpec((bb, tt, D), lambda b, t: (b, t, 0)),
                  pl.BlockSpec((1, D), lambda b, t: (0, 0)),
                  pl.BlockSpec((bb, 1, D), lambda b, t: (b, 0, 0)),
                  pl.BlockSpec((bb, 1, D), lambda b, t: (b, 0, 0))],
        out_specs=pl.BlockSpec((bb, tt, D), lambda b, t: (b, t, 0)),
        out_shape=jax.ShapeDtypeStruct((B, T, D), BF16),
        compiler_params=_cparams(("parallel", "parallel")),
        name="modnorm",
    )(x, g.reshape(1, D), scale, shift)


def _mm_kernel(x_ref, w_ref, o_ref):
    bb, tt, K = x_ref.shape
    acc = jnp.dot(x_ref[...].reshape(bb * tt, K), w_ref[...], preferred_element_type=F32)
    o_ref[...] = acc.reshape(o_ref.shape).astype(o_ref.dtype)


def _mm_res_kernel(x_ref, w_ref, res_ref, gate_ref, o_ref):
    bb, tt, K = x_ref.shape
    acc = jnp.dot(x_ref[...].reshape(bb * tt, K), w_ref[...], preferred_element_type=F32)
    o_ref[...] = res_ref[...] + gate_ref[...] * acc.reshape(o_ref.shape)


def _mm(x, w, tn, out_dtype=F32, res=None, gate=None, name="mm"):
    B, T, K = x.shape
    N = w.shape[1]
    bb, tt = _row_blocks(B, T, 1024)
    in_specs = [pl.BlockSpec((bb, tt, K), lambda b, t, j: (b, t, 0)),
                pl.BlockSpec((K, tn), lambda b, t, j: (0, j))]
    args = [x, w]
    kern = _mm_kernel
    if res is not None:
        in_specs += [pl.BlockSpec((bb, tt, tn), lambda b, t, j: (b, t, j)),
                     pl.BlockSpec((bb, 1, tn), lambda b, t, j: (b, 0, j))]
        args += [res, gate]
        kern = _mm_res_kernel
    return pl.pallas_call(
        kern,
        grid=(B // bb, T // tt, N // tn),
        in_specs=in_specs,
        out_specs=pl.BlockSpec((bb, tt, tn), lambda b, t, j: (b, t, j)),
        out_shape=jax.ShapeDtypeStruct((B, T, N), out_dtype),
        compiler_params=_cparams(("parallel", "parallel", "arbitrary")),
        name=name,
    )(*args)


def _merge_kernel(hd_ref, ya_ref, yb_ref, yc_ref, yd_ref, wmu_ref, bmg_ref, wb_ref, o_ref):
    bb, tt, _ = hd_ref.shape
    rows = bb * tt
    hd = hd_ref[...].reshape(rows, MERGE_RANK).astype(BF16)
    acc = None
    for i, y_ref in enumerate((ya_ref, yb_ref, yc_ref, yd_ref)):
        gate = _sigmoid(jnp.dot(hd, wmu_ref[i], preferred_element_type=F32) + bmg_ref[i])
        term = gate * jnp.dot(y_ref[...].reshape(rows, BRANCH_W), wb_ref[i], preferred_element_type=F32)
        acc = term if acc is None else acc + term
    o_ref[...] = acc.reshape(o_ref.shape).astype(o_ref.dtype)


def _merge(p_small, ys, wmu, bmg, wb, tn=512):
    B, T, _ = p_small.shape
    D = wb.shape[2]
    bb, tt = _row_blocks(B, T, 512)
    yspec = pl.BlockSpec((bb, tt, BRANCH_W), lambda b, t, j: (b, t, 0))
    return pl.pallas_call(
        _merge_kernel,
        grid=(B // bb, T // tt, D // tn),
        in_specs=[pl.BlockSpec((bb, tt, MERGE_RANK), lambda b, t, j: (b, t, 1)),
                  yspec, yspec, yspec, yspec,
                  pl.BlockSpec((4, MERGE_RANK, tn), lambda b, t, j: (0, 0, j)),
                  pl.BlockSpec((4, 1, tn), lambda b, t, j: (0, 0, j)),
                  pl.BlockSpec((4, BRANCH_W, tn), lambda b, t, j: (0, 0, j))],
        out_specs=pl.BlockSpec((bb, tt, tn), lambda b, t, j: (b, t, j)),
        out_shape=jax.ShapeDtypeStruct((B, T, D), BF16),
        compiler_params=_cparams(("parallel", "parallel", "arbitrary")),
        name="merge",
    )(p_small, *ys, wmu, bmg, wb)


def _route_kernel(x_ref, g_ref, sc_ref, sh_ref, wh_ref, wl_ref, br_ref, h_ref, r_ref):
    bb, tt, D = x_ref.shape
    rows = bb * tt
    h = _modnorm(x_ref[...], g_ref[...], sc_ref[...], sh_ref[...])
    h_ref[...] = h.astype(h_ref.dtype)
    h = h.reshape(rows, D)
    hh, hl = _split2(h)
    f = lambda u, v: lax.dot_general(u, v, NT_DIMS, preferred_element_type=F32)
    lg = f(wh_ref[...], hh) + f(wh_ref[...], hl) + f(wl_ref[...], hh) + br_ref[...]
    gl = [lg[i:i + 1, :] for i in range(N_GROUPS)]
    el = [lg[N_GROUPS + i:N_GROUPS + i + 1, :] for i in range(N_EXPERTS)]
    best = gl[0]
    gidx = jnp.zeros(best.shape, jnp.int32)
    for i in range(1, N_GROUPS):
        m = gl[i] > best
        best = jnp.where(m, gl[i], best)
        gidx = jnp.where(m, i, gidx)
    den = None
    for i in range(N_GROUPS):
        e = jnp.exp(gl[i] - best)
        den = e if den is None else den + e
    p_grp = 1.0 / den
    e_in = []
    for j in range(EXP_PER_GROUP):
        v = el[j]
        for g in range(1, N_GROUPS):
            v = jnp.where(gidx == g, el[g * EXP_PER_GROUP + j], v)
        e_in.append(v)
    t1 = e_in[0]
    i1 = jnp.zeros(best.shape, jnp.int32)
    for j in range(1, EXP_PER_GROUP):
        m = e_in[j] > t1
        t1 = jnp.where(m, e_in[j], t1)
        i1 = jnp.where(m, j, i1)
    cand = [jnp.where(i1 == j, -jnp.inf, e_in[j]) for j in range(EXP_PER_GROUP)]
    t2 = cand[0]
    i2 = jnp.zeros(best.shape, jnp.int32)
    for j in range(1, EXP_PER_GROUP):
        m = cand[j] > t2
        t2 = jnp.where(m, cand[j], t2)
        i2 = jnp.where(m, j, i2)
    e2 = jnp.exp(t2 - t1)
    w1 = p_grp / (1.0 + e2)
    w2 = p_grp * e2 / (1.0 + e2)
    r_ref[0:1, :] = (gidx * EXP_PER_GROUP + i1).astype(F32)
    r_ref[1:2, :] = (gidx * EXP_PER_GROUP + i2).astype(F32)
    r_ref[2:3, :] = w1
    r_ref[3:4, :] = w2
    r_ref[4:8, :] = jnp.zeros((4, rows), F32)


def _route(x, g, scale, shift, wr_hi, wr_lo, br):
    B, T, D = x.shape
    bb, tt = _row_blocks(B, T, 512)
    nT = T // tt
    rows = bb * tt
    return pl.pallas_call(
        _route_kernel,
        grid=(B // bb, nT),
        in_specs=[pl.BlockSpec((bb, tt, D), lambda b, t: (b, t, 0)),
                  pl.BlockSpec((1, D), lambda b, t: (0, 0)),
                  pl.BlockSpec((bb, 1, D), lambda b, t: (b, 0, 0)),
                  pl.BlockSpec((bb, 1, D), lambda b, t: (b, 0, 0)),
                  pl.BlockSpec((128, D), lambda b, t: (0, 0)),
                  pl.BlockSpec((128, D), lambda b, t: (0, 0)),
                  pl.BlockSpec((128, 1), lambda b, t: (0, 0))],
        out_specs=[pl.BlockSpec((bb, tt, D), lambda b, t: (b, t, 0)),
                   pl.BlockSpec((8, rows), lambda b, t: (0, b * nT + t))],
        out_shape=[jax.ShapeDtypeStruct((B, T, D), BF16),
                   jax.ShapeDtypeStruct((8, B * T), F32)],
        compiler_params=_cparams(("parallel", "parallel")),
        name="route",
    )(x, g.reshape(1, D), scale, shift, wr_hi, wr_lo, br)


def _experts_kernel(te_ref, tv_ref, x_ref, wg_ref, wu_ref, wd_ref, o_ref):
    t = pl.program_id(0)
    f = pl.program_id(1)

    @pl.when(f == 0)
    def _():
        o_ref[...] = jnp.zeros(o_ref.shape, F32)

    @pl.when(tv_ref[t] > 0)
    def _():
        x = x_ref[...]
        hg = jnp.dot(x, wg_ref[...].astype(BF16), preferred_element_type=F32)
        hu = jnp.dot(x, wu_ref[...].astype(BF16), preferred_element_type=F32)
        hid = (_silu(hg) * hu).astype(BF16)
        o_ref[...] += jnp.dot(hid, wd_ref[...].astype(BF16), preferred_element_type=F32)


def _experts(tile_expert, tile_valid, xs, w_gate, w_up, w_down, tm, fc=256):
    Mpad, D = xs.shape
    nf = D_FF // fc
    n_tiles = Mpad // tm

    def w_in_map(t, f, te, tv):
        return (te[t], 0, jnp.where(tv[t] > 0, f, nf - 1))

    def w_out_map(t, f, te, tv):
        return (te[t], jnp.where(tv[t] > 0, f, nf - 1), 0)

    grid_spec = pltpu.PrefetchScalarGridSpec(
        num_scalar_prefetch=2,
        grid=(n_tiles, nf),
        in_specs=[pl.BlockSpec((tm, D), lambda t, f, te, tv: (t, 0)),
                  pl.BlockSpec((None, D, fc), w_in_map),
                  pl.BlockSpec((None, D, fc), w_in_map),
                  pl.BlockSpec((None, fc, D), w_out_map)],
        out_specs=pl.BlockSpec((tm, D), lambda t, f, te, tv: (t, 0)),
    )
    return pl.pallas_call(
        _experts_kernel,
        grid_spec=grid_spec,
        out_shape=jax.ShapeDtypeStruct((Mpad, D), F32),
        compiler_params=_cparams(("arbitrary", "arbitrary")),
        name="experts",
    )(tile_expert, tile_valid, xs, w_gate, w_up, w_down)


def _combine_kernel(x_ref, gate_ref, g0_ref, g1_ref, w0_ref, w1_ref, o_ref):
    o_ref[...] = x_ref[...] + gate_ref[...] * (w0_ref[...] * g0_ref[...] + w1_ref[...] * g1_ref[...])


def _combine_final_kernel(x_ref, gate_ref, g0_ref, g1_ref, w0_ref, w1_ref, gf_ref, o_ref):
    x = x_ref[...] + gate_ref[...] * (w0_ref[...] * g0_ref[...] + w1_ref[...] * g1_ref[...])
    ms = jnp.mean(x * x, axis=-1, keepdims=True)
    o_ref[...] = x * lax.rsqrt(ms + EPS) * gf_ref[...]


def _combine(x, gate, g0, g1, w0, w1, g_final=None):
    B, T, D = x.shape
    bb, tt = _row_blocks(B, T, 256)
    big = pl.BlockSpec((bb, tt, D), lambda b, t: (b, t, 0))
    col = pl.BlockSpec((bb, tt, 1), lambda b, t: (b, t, 0))
    in_specs = [big, pl.BlockSpec((bb, 1, D), lambda b, t: (b, 0, 0)), big, big, col, col]
    args = [x, gate, g0, g1, w0, w1]
    kern = _combine_kernel
    if g_final is not None:
        in_specs.append(pl.BlockSpec((1, D), lambda b, t: (0, 0)))
        args.append(g_final.reshape(1, D))
        kern = _combine_final_kernel
    return pl.pallas_call(
        kern,
        grid=(B // bb, T // tt),
        in_specs=in_specs,
        out_specs=big,
        out_shape=jax.ShapeDtypeStruct((B, T, D), F32),
        compiler_params=_cparams(("parallel", "parallel")),
        name="combine",
    )(*args)


def _moe(x1, gate2, g_norm2, scale2, shift2, wr_hi, wr_lo, br, w_e_gate, w_e_up, w_e_down, g_final):
    B, T, D = x1.shape
    M = B * T
    h2, route = _route(x1, g_norm2, scale2, shift2, wr_hi, wr_lo, br)
    tm = 512 if M >= 4096 else 128
    eid = route[0:2].astype(jnp.int32).reshape(2 * M)
    wts = route[2:4]
    onehot = (eid[:, None] == jnp.arange(N_EXPERTS, dtype=jnp.int32)[None, :]).astype(jnp.int32)
    csum = jnp.cumsum(onehot, axis=0)
    counts = csum[-1]
    rank = jnp.sum((csum - onehot) * onehot, axis=1)
    padded = ((counts + tm - 1) // tm) * tm
    starts = jnp.cumsum(padded) - padded
    pos = starts[eid] + rank
    n_tiles = (2 * M) // tm + N_EXPERTS
    Mpad = n_tiles * tm
    token = jnp.tile(jnp.arange(M, dtype=jnp.int32), 2)
    token_of_pos = jnp.zeros((Mpad,), jnp.int32).at[pos].set(token)
    tile_start = jnp.arange(n_tiles, dtype=jnp.int32) * tm
    ends = starts + padded
    tile_expert = jnp.sum((tile_start[:, None] >= ends[None, :]).astype(jnp.int32), axis=1)
    tile_valid = (tile_expert < N_EXPERTS).astype(jnp.int32)
    last_e = jnp.max(jnp.where(counts > 0, jnp.arange(N_EXPERTS, dtype=jnp.int32), 0))
    tile_expert = jnp.where(tile_valid > 0, tile_expert, last_e).astype(jnp.int32)
    xs = jnp.take(h2.reshape(M, D), token_of_pos, axis=0)
    out = _experts(tile_expert, tile_valid, xs, w_e_gate, w_e_up, w_e_down, tm)
    g0 = jnp.take(out, pos[:M], axis=0).reshape(B, T, D)
    g1 = jnp.take(out, pos[M:], axis=0).reshape(B, T, D)
    w0 = wts[0].reshape(B, T, 1)
    w1 = wts[1].reshape(B, T, 1)
    return _combine(x1, gate2, g0, g1, w0, w1, g_final)


def _conv4(ext_ref, x, cw_ref, cb_ref, rows):
    ext_ref[8:8 + rows, :] = x
    out = cb_ref[...] + ext_ref[5:5 + rows, :] * cw_ref[0:1, :]
    out = out + ext_ref[6:6 + rows, :] * cw_ref[1:2, :]
    out = out + ext_ref[7:7 + rows, :] * cw_ref[2:3, :]
    return out + x * cw_ref[3:4, :]


def _lru_kernel(lx_ref, lg_ref, buf_ref, h0_ref, cw_ref, cb_ref, wa_ref, ba_ref, wx_ref, bx_ref, lam_ref,
                y_ref, nbuf_ref, hl_ref, ext_scr, a_scr, u_scr, h_scr, hc_scr):
    t = pl.program_id(1)
    Tc = lx_ref.shape[0]

    @pl.when(t == 0)
    def _():
        ext_scr[0:5, :] = jnp.zeros((5, BRANCH_W), F32)
        ext_scr[5:8, :] = buf_ref[...]
        hc_scr[...] = h0_ref[...]

    xc = _conv4(ext_scr, lx_ref[...], cw_ref, cb_ref, Tc)
    tail = ext_scr[Tc + 5:Tc + 8, :]
    ext_scr[5:8, :] = tail
    nbuf_ref[...] = tail

    def blockdiag(w_ref):
        return jnp.concatenate(
            [_dot(xc[:, n * LRU_BW:(n + 1) * LRU_BW], w_ref[n]) for n in range(LRU_BLOCKS)], axis=1)

    r = _sigmoid(blockdiag(wa_ref) + ba_ref[...])
    i = _sigmoid(blockdiag(wx_ref) + bx_ref[...])
    log_a = LRU_C * r * _log_sigmoid(lam_ref[...])
    a_scr[...] = jnp.exp(log_a)
    u_scr[...] = jnp.sqrt(-jnp.tanh(log_a) * (jnp.exp(2.0 * log_a) + 1.0)) * (i * xc)

    def body(s, h):
        h = a_scr[pl.ds(s, 1), :] * h + u_scr[pl.ds(s, 1), :]
        h_scr[pl.ds(s, 1), :] = h
        return h

    h = lax.fori_loop(0, Tc, body, hc_scr[...], unroll=8)
    hc_scr[...] = h
    hl_ref[...] = h
    y_ref[...] = (_gelu_tanh(lg_ref[...]) * h_scr[...]).astype(y_ref.dtype)


def _lru(p_lru, buf, h0, conv_w, conv_b, wa, ba, wx, bx, lam):
    B, T, _ = p_lru.shape
    W = BRANCH_W
    Tc = min(T, 256)
    row = lambda a: a.reshape(1, W)
    const = lambda shape: pl.BlockSpec(shape, lambda b, t: (0,) * len(shape))
    y, nbuf, hl = pl.pallas_call(
        _lru_kernel,
        grid=(B, T // Tc),
        in_specs=[pl.BlockSpec((None, Tc, W), lambda b, t: (b, t, 0)),
                  pl.BlockSpec((None, Tc, W), lambda b, t: (b, t, 1)),
                  pl.BlockSpec((None, CONV_W - 1, W), lambda b, t: (b, 0, 0)),
                  pl.BlockSpec((None, 1, W), lambda b, t: (b, 0, 0)),
                  const((CONV_W, W)), const((1, W)),
                  const((LRU_BLOCKS, LRU_BW, LRU_BW)), const((1, W)),
                  const((LRU_BLOCKS, LRU_BW, LRU_BW)), const((1, W)), const((1, W))],
        out_specs=[pl.BlockSpec((None, Tc, W), lambda b, t: (b, t, 0)),
                   pl.BlockSpec((None, CONV_W - 1, W), lambda b, t: (b, 0, 0)),
                   pl.BlockSpec((None, 1, W), lambda b, t: (b, 0, 0))],
        out_shape=[jax.ShapeDtypeStruct((B, T, W), BF16),
                   jax.ShapeDtypeStruct((B, CONV_W - 1, W), F32),
                   jax.ShapeDtypeStruct((B, 1, W), F32)],
        scratch_shapes=[pltpu.VMEM((Tc + 8, W), F32), pltpu.VMEM((Tc, W), F32), pltpu.VMEM((Tc, W), F32),
                        pltpu.VMEM((Tc, W), F32), pltpu.VMEM((1, W), F32)],
        compiler_params=_cparams(("parallel", "arbitrary")),
        name="rglru",
    )(p_lru, p_lru, buf, h0.reshape(B, 1, W), conv_w, row(conv_b), wa, row(ba), wx, row(bx), row(lam))
    return y, nbuf, hl.reshape(B, W)


def _gla_kernel(q_ref, k_ref, v_ref, g_ref, ga_ref, s0_ref, wup_ref, wupt_ref, b_ref, bt_ref, gn_ref,
                y_ref, s_ref, s_scr):
    t = pl.program_id(1)
    nT = pl.num_programs(1)
    Tv = q_ref.shape[0]
    C = max(Tv, CHUNK)

    @pl.when(t == 0)
    def _():
        s_scr[...] = s0_ref[...]

    q = _pad_rows(q_ref[...], C) * (GLA_DK ** -0.5)
    k = _pad_rows(k_ref[...], C)
    v = _pad_rows(v_ref[...], C)
    ga = _pad_rows(ga_ref[...], C)
    la =_log_sigmoid(_dot_hl(ga, wup_ref[...]) + b_ref[...]) / GLA_NORMALIZER
    lat = _log_sigmoid(_dot_hl_nt(wupt_ref[...], ga) + bt_ref[...]) / GLA_NORMALIZER
    if Tv < C:
        la = jnp.where(lax.broadcasted_iota(jnp.int32, la.shape, 0) < Tv, la, 0.0)
        lat = jnp.where(lax.broadcasted_iota(jnp.int32, lat.shape, 1) < Tv, lat, 0.0)
    tri = jnp.where(_tri(C), 1.0, 0.0).astype(BF16)
    bc = _dot_3x(tri, la)
    b_last = bc[C - 1:C, :]
    b_mid = bc[C // 2:C // 2 + 1, :]
    b_last_col = jnp.sum(lat, axis=1, keepdims=True)
    q_in = q * jnp.exp(bc)
    q_att = q * jnp.exp(bc - b_mid)
    k_att = k * jnp.exp(b_mid - bc)
    k_dec = k * jnp.exp(b_last - bc)
    causal = _tri(C)
    gn = gn_ref[...]
    outs = []
    for h in range(GLA_HEADS):
        ks = slice(h * GLA_DK, (h + 1) * GLA_DK)
        vs = slice(h * GLA_DV, (h + 1) * GLA_DV)
        vh = v[:, vs]
        S = s_scr[h]
        att = jnp.where(causal, _dot_nt(q_att[:, ks], k_att[:, ks]), 0.0)
        o = _dot(att, vh) + _dot(q_in[:, ks], S)
        s_scr[h] = S * jnp.exp(b_last_col[ks, :]) + _dot_tn(k_dec[:, ks], vh)
        o = o * lax.rsqrt(jnp.mean(o * o, axis=-1, keepdims=True) + EPS) * gn
        outs.append(o)
    o = jnp.concatenate(outs, axis=1)[0:Tv]
    y_ref[...] = (o * _silu(g_ref[...])).astype(y_ref.dtype)

    @pl.when(t == nT - 1)
    def _():
        s_ref[...] = s_scr[...]


def _gla(p_gla, p_small, S0, w_up, b_a, g_norm):
    B, T, _ = p_gla.shape
    Tv = min(T, CHUNK)
    const = lambda shape: pl.BlockSpec(shape, lambda b, t: (0,) * len(shape))
    st = (GLA_HEADS, GLA_DK, GLA_DV)
    w_up_pad = jnp.concatenate([w_up, jnp.zeros((128 - GLA_RANK, GLA_QK), F32)], axis=0)
    y, S = pl.pallas_call(
        _gla_kernel,
        grid=(B, T // Tv),
        in_specs=[pl.BlockSpec((None, Tv, GLA_QK), lambda b, t: (b, t, 0)),
                  pl.BlockSpec((None, Tv, GLA_QK), lambda b, t: (b, t, 1)),
                  pl.BlockSpec((None, Tv, BRANCH_W), lambda b, t: (b, t, 1)),
                  pl.BlockSpec((None, Tv, BRANCH_W), lambda b, t: (b, t, 2)),
                  pl.BlockSpec((None, Tv, 128), lambda b, t: (b, t, 0)),
                  pl.BlockSpec((None,) + st, lambda b, t: (b, 0, 0, 0)),
                  const((128, GLA_QK)), const((GLA_QK, 128)),
                  const((1, GLA_QK)), const((GLA_QK, 1)), const((1, GLA_DV))],
        out_specs=[pl.BlockSpec((None, Tv, BRANCH_W), lambda b, t: (b, t, 0)),
                   pl.BlockSpec((None,) + st, lambda b, t: (b, 0, 0, 0))],
        out_shape=[jax.ShapeDtypeStruct((B, T, BRANCH_W), BF16),
                   jax.ShapeDtypeStruct((B,) + st, F32)],
        scratch_shapes=[pltpu.VMEM(st, F32)],
        compiler_params=_cparams(("parallel", "arbitrary")),
        name="gla",
    )(p_gla, p_gla, p_gla, p_gla, p_small, S0, w_up_pad, w_up_pad.T, b_a.reshape(1, GLA_QK),
      b_a.reshape(GLA_QK, 1), g_norm.reshape(1, GLA_DV))
    return y, S


def _mlstm_kernel(qk_ref, v_ref, o_ref, if_ref, ift_ref, buf_ref, c0_ref, n0_ref, m0_ref,
                  cw_ref, cb_ref, bif_ref, bift_ref, ng_ref,
                  y_ref, nbuf_ref, c_ref, n_ref, m_ref, ext_scr, c_scr, n_scr, m_scr):
    t = pl.program_id(1)
    nT = pl.num_programs(1)
    Tv = qk_ref.shape[0]
    C = max(Tv, CHUNK)
    W2 = 2 * BRANCH_W

    @pl.when(t == 0)
    def _():
        ext_scr[0:5, :] = jnp.zeros((5, W2), F32)
        ext_scr[5:8, :] = buf_ref[...]
        c_scr[...] = c0_ref[...]
        n_scr[...] = n0_ref[...]
        m_scr[...] = jnp.zeros(m_scr.shape, F32)
        m_scr[0:1, 0:ML_HEADS] = m0_ref[...]

    qk = _conv4(ext_scr, qk_ref[...], cw_ref, cb_ref, Tv)
    tail = ext_scr[Tv + 5:Tv + 8, :]
    ext_scr[5:8, :] = tail
    nbuf_ref[...] = tail
    qk = _pad_rows(_silu(qk), C)
    q = qk[:, 0:BRANCH_W]
    k = qk[:, BRANCH_W:W2] * (ML_DH ** -0.5)
    v = _pad_rows(v_ref[...], C)

    gates = _pad_rows(if_ref[...], C) + bif_ref[...]
    i_col = gates
    lf_col = _log_sigmoid(gates)
    gates_t = _pad_rows(ift_ref[...] + bift_ref[...], 16)
    if Tv < C:
        gates_t = jnp.concatenate([gates_t, jnp.zeros((16, C - Tv), F32)], axis=1)
    i_row = gates_t
    lf_row = _log_sigmoid(gates_t)
    if Tv < C:
        rmask = lax.broadcasted_iota(jnp.int32, (C, 128), 0) < Tv
        cmask = lax.broadcasted_iota(jnp.int32, (16, C), 1) < Tv
        i_col = jnp.where(rmask, i_col, NEG_BIG)
        lf_col = jnp.where(rmask, lf_col, 0.0)
        i_row = jnp.where(cmask, i_row, NEG_BIG)
        lf_row = jnp.where(cmask, lf_row, 0.0)
    causal = _tri(C)
    tri = jnp.where(causal, 1.0, 0.0).astype(BF16)
    triu = jnp.where(_tri(C, upper=True), 1.0, 0.0).astype(BF16)
    F_col = _dot_3x(tri, lf_col)
    F_row = _dot_x3(lf_row, triu)
    ng = ng_ref[...]
    outs = []
    for h in range(ML_HEADS):
        hs = slice(h * ML_DH, (h + 1) * ML_DH)
        qh, kh, vh = q[:, hs], k[:, hs], v[:, hs]
        Fc = F_col[:, 16 + ML_HEADS + h:17 + ML_HEADS + h]
        ic = i_col[:, 16 + h:17 + h]
        m_prev = m_scr[0:1, h:h + 1]
        Fr = F_row[ML_HEADS + h:ML_HEADS + h + 1, :]
        log_d = jnp.where(causal, Fc - Fr + i_row[h:h + 1, :], NEG_BIG)
        log_inter = Fc + m_prev
        m_t = jnp.maximum(log_inter, jnp.max(log_d, axis=1, keepdims=True))
        s = _dot_nt(qh, kh) * jnp.exp(log_d - m_t)
        inter = jnp.exp(log_inter - m_t)
        Ch = c_scr[h]
        nh = n_scr[h:h + 1, :]
        num = _dot(s, vh) + inter * _dot(qh, Ch)
        den = jnp.sum(s, axis=1, keepdims=True) + inter * jnp.sum(qh * nh, axis=1, keepdims=True)
        hh = num / jnp.maximum(jnp.abs(den), jnp.exp(-m_t))
        m_new = m_t[C - 1:C, :]
        F_last = Fc[C - 1:C, :]
        decay = jnp.exp(F_last + m_prev - m_new)
        w_s = jnp.exp(F_last - Fc + ic - m_new)
        kw = kh * w_s
        c_scr[h] = decay * Ch + _dot_tn(kw, vh)
        n_scr[h:h + 1, :] = decay * nh + jnp.sum(kw, axis=0, keepdims=True)
        m_scr[0:1, h:h + 1] = m_new
        hh = hh * lax.rsqrt(jnp.mean(hh * hh, axis=-1, keepdims=True) + EPS) * ng[:, hs]
        outs.append(hh)
    hcat = jnp.concatenate(outs, axis=1)[0:Tv]
    y_ref[...] = (hcat * _sigmoid(o_ref[...])).astype(y_ref.dtype)

    @pl.when(t == nT - 1)
    def _():
        c_ref[...] = c_scr[...]
        n_ref[...] = n_scr[...]
        m_ref[...] = m_scr[0:1, 0:ML_HEADS]


def _mlstm(p_ml, p_small, buf, C0, n0, m0, conv_w, conv_b, b_if, norm_g):
    B, T, _ = p_ml.shape
    W = BRANCH_W
    Tv = min(T, CHUNK)
    ift = jnp.swapaxes(p_small[:, :, 16:16 + 2 * ML_HEADS], 1, 2)
    const = lambda shape: pl.BlockSpec(shape, lambda b, t: (0,) * len(shape))
    cst = (ML_HEADS, ML_DH, ML_DH)
    y, nbuf, Cn, nn, mn = pl.pallas_call(
        _mlstm_kernel,
        grid=(B, T // Tv),
        in_specs=[pl.BlockSpec((None, Tv, 2 * W), lambda b, t: (b, t, 0)),
                  pl.BlockSpec((None, Tv, W), lambda b, t: (b, t, 2)),
                  pl.BlockSpec((None, Tv, W), lambda b, t: (b, t, 3)),
                  pl.BlockSpec((None, Tv, 128), lambda b, t: (b, t, 0)),
                  pl.BlockSpec((None, 2 * ML_HEADS, Tv), lambda b, t: (b, 0, t)),
                  pl.BlockSpec((None, CONV_W - 1, 2 * W), lambda b, t: (b, 0, 0)),
                  pl.BlockSpec((None,) + cst, lambda b, t: (b, 0, 0, 0)),
                  pl.BlockSpec((None, ML_HEADS, ML_DH), lambda b, t: (b, 0, 0)),
                  pl.BlockSpec((None, 1, ML_HEADS), lambda b, t: (b, 0, 0)),
                  const((CONV_W, 2 * W)), const((1, 2 * W)),
                  const((1, 128)), const((2 * ML_HEADS, 1)), const((1, W))],
        out_specs=[pl.BlockSpec((None, Tv, W), lambda b, t: (b, t, 0)),
                   pl.BlockSpec((None, CONV_W - 1, 2 * W), lambda b, t: (b, 0, 0)),
                   pl.BlockSpec((None,) + cst, lambda b, t: (b, 0, 0, 0)),
                   pl.BlockSpec((None, ML_HEADS, ML_DH), lambda b, t: (b, 0, 0)),
                   pl.BlockSpec((None, 1, ML_HEADS), lambda b, t: (b, 0, 0))],
        out_shape=[jax.ShapeDtypeStruct((B, T, W), BF16),
                   jax.ShapeDtypeStruct((B, CONV_W - 1, 2 * W), F32),
                   jax.ShapeDtypeStruct((B,) + cst, F32),
                   jax.ShapeDtypeStruct((B, ML_HEADS, ML_DH), F32),
                   jax.ShapeDtypeStruct((B, 1, ML_HEADS), F32)],
        scratch_shapes=[pltpu.VMEM((Tv + 8, 2 * W), F32), pltpu.VMEM(cst, F32),
                        pltpu.VMEM((ML_HEADS, ML_DH), F32), pltpu.VMEM((8, 128), F32)],
        compiler_params=_cparams(("parallel", "arbitrary")),
        name="mlstm",
    )(p_ml, p_ml, p_ml, p_small, ift, buf, C0, n0, m0.reshape(B, 1, ML_HEADS),
      conv_w, conv_b.reshape(1, 2 * W),
      jnp.zeros((1, 128), F32).at[0, 16:16 + 2 * ML_HEADS].set(b_if), b_if.reshape(2 * ML_HEADS, 1),
      norm_g.reshape(1, W))
    return y, nbuf, Cn, nn, mn.reshape(B, ML_HEADS)


def _rwkv_kernel(p_ref, sh_ref, s0_ref, mu_ref, w0_ref, wup_ref, a0_ref, aup_ref, gup_ref,
                 kk_ref, ka_ref, rk_ref, lng_ref, lnb_ref,
                 y_ref, shn_ref, s_ref,
                 ext_scr, nkk_scr, w_scr, kka_scr, k2_scr, r_scr, v_scr, g_scr, bon_scr,
                 d_scr, vcol_scr, st_scr, yr_scr):
    t = pl.program_id(1)
    nT = pl.num_programs(1)
    Tb = p_ref.shape[0]
    W = BRANCH_W
    nsub = Tb // RW_SUB

    @pl.when(t == 0)
    def _():
        ext_scr[0:7, :] = jnp.zeros((7, RW_NCOLS), F32)
        ext_scr[7:8, :] = sh_ref[...]
        st_scr[...] = s0_ref[...]

    ones_blk = _seg_ones(256, RW_DH)
    p = p_ref[...]
    ext_scr[8:8 + Tb, :] = p
    prev = ext_scr[7:7 + Tb, :]
    last = p[Tb - 1:Tb, :]
    ext_scr[7:8, :] = last
    shn_ref[...] = last
    pm = p + (prev - p) * mu_ref[...]
    r = pm[:, 0:W]
    k = pm[:, W:2 * W]
    v = pm[:, 2 * W:3 * W]
    wa_d = pm[:, 3 * W:3 * W + 128]
    gd = pm[:, 3 * W + 128:3 * W + 256]
    log_w = -RW_DECAY_SCALE * _sigmoid(w0_ref[...] + _dot(jnp.tanh(wa_d), wup_ref[...]))
    a = _sigmoid(a0_ref[...] + _dot(wa_d, aup_ref[...]))
    g_scr[...] = _dot(_sigmoid(gd), gup_ref[...])
    kkr = k * kk_ref[...]
    kk = kkr / jnp.maximum(jnp.sqrt(_segsum3(kkr * kkr, ones_blk)), 1e-12)
    k2 = k * (1.0 + (a - 1.0) * ka_ref[...])
    bon_scr[...] = _segsum3(r * k2 * rk_ref[...], ones_blk) * v
    nkk_scr[...] = -kk
    w_scr[...] = jnp.exp(log_w)
    kka_scr[...] = kk * a
    k2_scr[...] = k2
    r_scr[...] = r
    v_scr[...] = v

    eye_t = (lax.broadcasted_iota(jnp.int32, (RW_DH, W), 0)
             == lax.broadcasted_iota(jnp.int32, (RW_DH, W), 1) % RW_DH)
    eye_f = jnp.where(eye_t, 1.0, 0.0)

    def sub_block(sb, carry):
        base = pl.multiple_of(sb * RW_SUB, RW_SUB)

        def build(s, c):
            vrow = v_scr[pl.ds(base + s, 1), :]
            d_scr[pl.ds(pl.multiple_of(s * RW_DH, RW_DH), RW_DH), :] = (eye_f * vrow).astype(BF16)
            return c

        lax.fori_loop(0, RW_SUB, build, 0)
        vcol_scr[...] = _segsum_bf16(d_scr[...], ones_blk)

        def step(s, c):
            row = base + s
            S = st_scr[...]
            t1 = (S * nkk_scr[pl.ds(row, 1), :]).astype(BF16)
            sa = _segsum_bf16(t1, ones_blk)
            vc = vcol_scr[pl.ds(pl.multiple_of(s * RW_DH, RW_DH), RW_DH), :]
            S = (S * w_scr[pl.ds(row, 1), :] + sa * kka_scr[pl.ds(row, 1), :]
                 + vc * k2_scr[pl.ds(row, 1), :])
            st_scr[...] = S
            yb = _segsum_bf16((S * r_scr[pl.ds(row, 1), :]).astype(BF16), ones_blk)
            yr_scr[pl.ds(row, 1), :] = jnp.sum(yb * eye_f, axis=0, keepdims=True)
            return c

        lax.fori_loop(0, RW_SUB, step, 0)
        return carry

    lax.fori_loop(0, nsub, sub_block, 0)

    y = yr_scr[...]
    mean = _segsum3(y, ones_blk) * (1.0 / RW_DH)
    dlt = y - mean
    var = _segsum3(dlt * dlt, ones_blk) * (1.0 / RW_DH)
    yn = dlt * lax.rsqrt(var + RW_LN_EPS) * lng_ref[...] + lnb_ref[...]
    y_ref[...] = ((yn + bon_scr[...]) * g_scr[...]).astype(y_ref.dtype)

    @pl.when(t == nT - 1)
    def _():
        s_ref[...] = st_scr[...]


def _rwkv(p_rw, shift_prev, S0, mu, w0, w_up, a0, a_up, g_up, k_k, k_a, r_k, ln_g, ln_b):
    B, T, _ = p_rw.shape
    W = BRANCH_W
    Tb = min(T, 256)
    s0 = jnp.transpose(S0, (0, 2, 1, 3)).reshape(B, RW_DH, W)
    row = lambda a: a.reshape(1, -1)
    const = lambda shape: pl.BlockSpec(shape, lambda b, t: (0,) * len(shape))
    rows = lambda: pltpu.VMEM((Tb, W), F32)
    y, shn, S = pl.pallas_call(
        _rwkv_kernel,
        grid=(B, T // Tb),
        in_specs=[pl.BlockSpec((None, Tb, RW_NCOLS), lambda b, t: (b, t, 0)),
                  pl.BlockSpec((None, 1, RW_NCOLS), lambda b, t: (b, 0, 0)),
                  pl.BlockSpec((None, RW_DH, W), lambda b, t: (b, 0, 0)),
                  const((1, RW_NCOLS)), const((1, W)), const((128, W)), const((1, W)), const((128, W)),
                  const((128, W)), const((1, W)), const((1, W)), const((1, W)), const((1, W)), const((1, W))],
        out_specs=[pl.BlockSpec((None, Tb, W), lambda b, t: (b, t, 0)),
                   pl.BlockSpec((None, 1, RW_NCOLS), lambda b, t: (b, 0, 0)),
                   pl.BlockSpec((None, RW_DH, W), lambda b, t: (b, 0, 0))],
        out_shape=[jax.ShapeDtypeStruct((B, T, W), BF16),
                   jax.ShapeDtypeStruct((B, 1, RW_NCOLS), F32),
                   jax.ShapeDtypeStruct((B, RW_DH, W), F32)],
        scratch_shapes=[pltpu.VMEM((Tb + 8, RW_NCOLS), F32),
                        rows(), rows(), rows(), rows(), rows(), rows(), rows(), rows(),
                        pltpu.VMEM((RW_SUB * RW_DH, W), BF16), pltpu.VMEM((RW_SUB * RW_DH, W), F32),
                        pltpu.VMEM((RW_DH, W), F32), rows()],
        compiler_params=_cparams(("parallel", "arbitrary")),
        name="rwkv7",
    )(p_rw, shift_prev.reshape(B, 1, RW_NCOLS), s0, row(mu), row(w0),
      jnp.concatenate([w_up, jnp.zeros_like(w_up)], axis=0), row(a0),
      jnp.concatenate([jnp.zeros_like(a_up), a_up], axis=0), g_up,
      row(k_k), row(k_a), row(r_k), row(ln_g), row(ln_b))
    S = jnp.transpose(S.reshape(B, RW_DH, RW_HEADS, RW_DH), (0, 2, 1, 3))
    return y, shn.reshape(B, RW_NCOLS), S


def _prep_layer(l, w_in, w_mg_down, w_mg_up, b_mg, w_branch, w_out, w_route_g, b_route_g, w_route_e, b_route_e):
    W = BRANCH_W
    wi = w_in[l]
    offs = [0]
    for s in (GLA_QK, GLA_QK, W, W, GLA_RANK, RW_NCOLS, 2 * W, W, W, 2 * ML_HEADS, W, W):
        offs.append(offs[-1] + s)
    seg = lambda i: wi[:, offs[i]:offs[i + 1]]
    w_gla = jnp.concatenate([seg(0), seg(1), seg(2), seg(3)], axis=1).astype(BF16)
    w_rw = seg(5).astype(BF16)
    w_ml = jnp.concatenate([seg(6), seg(7), seg(8)], axis=1).astype(BF16)
    w_lru = jnp.concatenate([seg(10), seg(11)], axis=1).astype(BF16)
    pad = jnp.zeros((D_MODEL, 256 - GLA_RANK - 2 * ML_HEADS), F32)
    w_small = jnp.concatenate([seg(4), seg(9), pad, w_mg_down[l]], axis=1).astype(BF16)
    wmu = jnp.transpose(w_mg_up[l].reshape(MERGE_RANK, 4, D_MODEL), (1, 0, 2)).astype(BF16)
    bmg = b_mg[l].reshape(4, 1, D_MODEL)
    wr = jnp.concatenate([w_route_g[l], w_route_e[l]], axis=1).T
    wr = jnp.concatenate([wr, jnp.zeros((128 - wr.shape[0], D_MODEL), F32)], axis=0)
    wr_hi = wr.astype(BF16)
    wr_lo = (wr - wr_hi.astype(F32)).astype(BF16)
    br = jnp.concatenate([b_route_g[l], b_route_e[l], jnp.zeros((128 - 20,), F32)]).reshape(128, 1)
    return dict(w_gla=w_gla, w_rw=w_rw, w_ml=w_ml, w_lru=w_lru, w_small=w_small, wmu=wmu, bmg=bmg,
                wb=w_branch[l].astype(BF16), w_out=w_out[l].astype(BF16), wr_hi=wr_hi, wr_lo=wr_lo, br=br)


def _run_trunk(x, mod, states, lw, prep, g_final):
    n_layers = mod.shape[0]
    new_states = []
    for l in range(n_layers):
        pw = prep[l]
        st = [s[l] for s in states]
        gla_S, rw_S, rw_shift, ml_C, ml_n, ml_m, ml_conv, lru_h, lru_conv = st
        m = mod[l]
        shift1, scale1, gate1, shift2, scale2, gate2 = [m[:, i:i + 1, :] for i in range(6)]
        h = _modnorm_call(x, lw["g_norm1"][l], scale1, shift1)
        p_gla = _mm(h, pw["w_gla"], 512, name="proj_gla")
        p_rw = _mm(h, pw["w_rw"], 256, name="proj_rw")
        p_ml = _mm(h, pw["w_ml"], 512, name="proj_ml")
        p_lru = _mm(h, pw["w_lru"], 512, name="proj_lru")
        p_small = _mm(h, pw["w_small"], 512, name="proj_small")
        y_a, gla_S = _gla(p_gla, p_small, gla_S, lw["gla_w_up"][l], lw["gla_b"][l], lw["gla_g_norm"][l])
        y_b, rw_shift, rw_S = _rwkv(p_rw, rw_shift, rw_S, lw["rw_mu"][l], lw["rw_w0"][l], lw["rw_w_up"][l],
                                    lw["rw_a0"][l], lw["rw_a_up"][l], lw["rw_g_up"][l], lw["rw_k_k"][l],
                                    lw["rw_k_a"][l], lw["rw_r_k"][l], lw["rw_ln_g"][l], lw["rw_ln_b"][l])
        y_c, ml_conv, ml_C, ml_n, ml_m = _mlstm(p_ml, p_small, ml_conv, ml_C, ml_n, ml_m, lw["ml_conv_w"][l],
                                                lw["ml_conv_b"][l], lw["ml_b_if"][l], lw["ml_norm_g"][l])
        y_d, lru_conv, lru_h = _lru(p_lru, lru_conv, lru_h, lw["lru_conv_w"][l], lw["lru_conv_b"][l],
                                    lw["lru_wa"][l], lw["lru_ba"][l], lw["lru_wx"][l], lw["lru_bx"][l],
                                    lw["lru_lambda"][l])
        merged = _merge(p_small, (y_a, y_b, y_c, y_d), pw["wmu"], pw["bmg"], pw["wb"])
        x1 = _mm(merged, pw["w_out"], 512, res=x, gate=gate1, name="out_proj")
        x = _moe(x1, gate2, lw["g_norm2"][l], scale2, shift2, pw["wr_hi"], pw["wr_lo"], pw["br"],
                 lw["w_e_gate"][l], lw["w_e_up"][l], lw["w_e_down"][l],
                 g_final if l == n_layers - 1 else None)
        new_states.append((gla_S, rw_S, rw_shift, ml_C, ml_n, ml_m, ml_conv, lru_h, lru_conv))
    return x, [jnp.stack([st[i] for st in new_states]) for i in range(9)]


def kernel(x_prompt, x_sample, c_prompt, c_sample, state_gla_S, state_rwkv_S, state_rwkv_shift, state_mlstm_C, state_mlstm_n, state_mlstm_m, state_mlstm_conv, state_lru_h, state_lru_conv, w_ada, b_ada, g_norm1, g_norm2, w_in, gla_w_up, gla_b, gla_g_norm, rw_mu, rw_w0, rw_w_up, rw_a0, rw_a_up, rw_g_up, rw_k_k, rw_k_a, rw_r_k, rw_ln_g, rw_ln_b, ml_conv_w, ml_conv_b, ml_b_if, ml_norm_g, lru_conv_w, lru_conv_b, lru_wa, lru_ba, lru_wx, lru_bx, lru_lambda, w_branch, w_mg_down, w_mg_up, b_mg, w_out, w_route_g, b_route_g, w_route_e, b_route_e, w_e_gate, w_e_up, w_e_down, g_final):
    n_layers = w_ada.shape[0]
    Bp, Bs = x_prompt.shape[0], x_sample.shape[0]
    lw = dict(g_norm1=g_norm1, g_norm2=g_norm2, gla_w_up=gla_w_up, gla_b=gla_b, gla_g_norm=gla_g_norm,
              rw_mu=rw_mu, rw_w0=rw_w0, rw_w_up=rw_w_up, rw_a0=rw_a0, rw_a_up=rw_a_up, rw_g_up=rw_g_up,
              rw_k_k=rw_k_k, rw_k_a=rw_k_a, rw_r_k=rw_r_k, rw_ln_g=rw_ln_g, rw_ln_b=rw_ln_b,
              ml_conv_w=ml_conv_w, ml_conv_b=ml_conv_b, ml_b_if=ml_b_if, ml_norm_g=ml_norm_g,
              lru_conv_w=lru_conv_w, lru_conv_b=lru_conv_b, lru_wa=lru_wa, lru_ba=lru_ba, lru_wx=lru_wx,
              lru_bx=lru_bx, lru_lambda=lru_lambda, w_e_gate=w_e_gate, w_e_up=w_e_up, w_e_down=w_e_down)
    prep = [_prep_layer(l, w_in, w_mg_down, w_mg_up, b_mg, w_branch, w_out,
                        w_route_g, b_route_g, w_route_e, b_route_e) for l in range(n_layers)]
    nb = Bp + Bs
    rows = ((nb + 15) // 16) * 16
    c_all = jnp.concatenate([c_prompt, c_sample, jnp.zeros((rows - nb, D_MODEL), F32)], axis=0)
    mod = _ada_mod(c_all, w_ada, b_ada).reshape(n_layers, rows, 6, D_MODEL)
    mod_p = mod[:, 0:Bp]
    mod_s = mod[:, Bp:nb]

    def zeros(*s):
        return jnp.zeros((n_layers, Bp) + s, F32)

    zero_states = (zeros(GLA_HEADS, GLA_DK, GLA_DV), zeros(RW_HEADS, RW_DH, RW_DH), zeros(RW_NCOLS),
                   zeros(ML_HEADS, ML_DH, ML_DH), zeros(ML_HEADS, ML_DH), zeros(ML_HEADS),
                   zeros(CONV_W - 1, 2 * BRANCH_W), zeros(BRANCH_W), zeros(CONV_W - 1, BRANCH_W))
    sample_states = (state_gla_S, state_rwkv_S, state_rwkv_shift, state_mlstm_C, state_mlstm_n,
                     state_mlstm_m, state_mlstm_conv, state_lru_h, state_lru_conv)
    y_prompt, ps = _run_trunk(x_prompt, mod_p, zero_states, lw, prep, g_final)
    y_sample, ss = _run_trunk(x_sample, mod_s, sample_states, lw, prep, g_final)
    return (y_prompt, y_sample, *ps, *ss)
```

```python
import functools

import jax
import jax.numpy as jnp
from jax import lax
from jax.experimental import pallas as pl
from jax.experimental.pallas import tpu as pltpu

F32 = jnp.float32
BF16 = jnp.bfloat16

D_MODEL = 4096
BRANCH_W = D_MODEL // 4
EPS = 1e-6
GLA_HEADS = 4
GLA_DK = 128
GLA_DV = 256
GLA_QK = GLA_HEADS * GLA_DK
GLA_RANK = 16
GLA_NORMALIZER = 16.0
RW_DH = 64
RW_HEADS = BRANCH_W // RW_DH
RW_NCOLS = 3 * BRANCH_W + 256
RW_DECAY_SCALE = 0.606531
RW_LN_EPS = 64e-5
ML_HEADS = 4
ML_DH = BRANCH_W // ML_HEADS
CONV_W = 4
LRU_BLOCKS = 8
LRU_BW = BRANCH_W // LRU_BLOCKS
LRU_C = 8.0
N_GROUPS = 4
EXP_PER_GROUP = 4
N_EXPERTS = 16
D_FF = D_MODEL // 4
MERGE_RANK = 256

VMEM_LIMIT = 56 * 1024 * 1024
CHUNK = 128
NEG_BIG = -1e30

NT_DIMS = (((1,), (1,)), ((), ()))
TN_DIMS = (((0,), (0,)), ((), ()))


def _cparams(sem):
    return pltpu.CompilerParams(dimension_semantics=sem, vmem_limit_bytes=VMEM_LIMIT)


def _dot(a, b):
    return jnp.dot(a.astype(BF16), b.astype(BF16), preferred_element_type=F32)


def _dot_nt(a, b):
    return lax.dot_general(a.astype(BF16), b.astype(BF16), NT_DIMS, preferred_element_type=F32)


def _dot_tn(a, b):
    return lax.dot_general(a.astype(BF16), b.astype(BF16), TN_DIMS, preferred_element_type=F32)


def _split3(x):
    hi = x.astype(BF16)
    r1 = x - hi.astype(F32)
    mid = r1.astype(BF16)
    lo = (r1 - mid.astype(F32)).astype(BF16)
    return hi, mid, lo


def _split2(x):
    hi = x.astype(BF16)
    lo = (x - hi.astype(F32)).astype(BF16)
    return hi, lo


def _dot_x3(a, b_exact):
    hi, mid, lo = _split3(a)
    f = lambda u: jnp.dot(u, b_exact, preferred_element_type=F32)
    return f(hi) + f(mid) + f(lo)


def _dot_3x(a_exact, b):
    hi, mid, lo = _split3(b)
    f = lambda u: jnp.dot(a_exact, u, preferred_element_type=F32)
    return f(hi) + f(mid) + f(lo)


def _dot_hl(a, b):
    ah, al = _split2(a)
    bh, bl = _split2(b)
    f = lambda u, v: jnp.dot(u, v, preferred_element_type=F32)
    return f(ah, bh) + f(al, bh) + f(ah, bl)


def _dot_hl_nt(a, b):
    ah, al = _split2(a)
    bh, bl = _split2(b)
    f = lambda u, v: lax.dot_general(u, v, NT_DIMS, preferred_element_type=F32)
    return f(ah, bh) + f(al, bh) + f(ah, bl)


def _sigmoid(x):
    return 1.0 / (1.0 + jnp.exp(-x))


def _silu(x):
    return x * _sigmoid(x)


def _log_sigmoid(x):
    return jnp.minimum(x, 0.0) - jnp.log(1.0 + jnp.exp(-jnp.abs(x)))


def _gelu_tanh(x):
    return 0.5 * x * (1.0 + jnp.tanh(0.7978845608028654 * (x + 0.044715 * x * x * x)))


def _tri(n, upper=False):
    r = lax.broadcasted_iota(jnp.int32, (n, n), 0)
    c = lax.broadcasted_iota(jnp.int32, (n, n), 1)
    m = (r <= c) if upper else (r >= c)
    return m


def _seg_ones(width, seg):
    r = lax.broadcasted_iota(jnp.int32, (width, width), 0) // seg
    c = lax.broadcasted_iota(jnp.int32, (width, width), 1) // seg
    return jnp.where(r == c, 1.0, 0.0).astype(BF16)


def _segsum_bf16(x_bf16, ones_blk):
    bw = ones_blk.shape[0]
    n = x_bf16.shape[1] // bw
    return jnp.concatenate(
        [jnp.dot(x_bf16[:, g * bw:(g + 1) * bw], ones_blk, preferred_element_type=F32) for g in range(n)],
        axis=1)


def _segsum3(x, ones_blk):
    hi, mid, lo = _split3(x)
    return _segsum_bf16(hi, ones_blk) + _segsum_bf16(mid, ones_blk) + _segsum_bf16(lo, ones_blk)


def _pad_rows(x, rows):
    if x.shape[0] == rows:
        return x
    return jnp.concatenate([x, jnp.zeros((rows - x.shape[0],) + x.shape[1:], x.dtype)], axis=0)


def _ada_kernel(c_ref, w_ref, b_ref, o_ref):
    c = c_ref[...]
    o_ref[...] = _dot(_silu(c), w_ref[...]) + b_ref[...]


def _ada_mod(c_all, w_ada, b_ada, tn=512):
    n_layers, d, n = w_ada.shape
    rows = c_all.shape[0]
    return pl.pallas_call(
        _ada_kernel,
        grid=(n_layers, n // tn),
        in_specs=[pl.BlockSpec((rows, d), lambda l, j: (0, 0)),
                  pl.BlockSpec((None, d, tn), lambda l, j: (l, 0, j)),
                  pl.BlockSpec((None, 1, tn), lambda l, j: (l, 0, j))],
        out_specs=pl.BlockSpec((None, rows, tn), lambda l, j: (l, 0, j)),
        out_shape=jax.ShapeDtypeStruct((n_layers, rows, n), F32),
        compiler_params=_cparams(("arbitrary", "arbitrary")),
        name="ada_mod",
    )(c_all, w_ada, b_ada.reshape(n_layers, 1, n))


def _row_blocks(B, T, max_rows):
    if B == 1:
        return 1, min(T, max_rows)
    bb = max(1, min(B, max_rows // T))
    assert B % bb == 0
    return bb, T


def _modnorm(x, g, scale, shift):
    ms = jnp.mean(x * x, axis=-1, keepdims=True)
    h = x * lax.rsqrt(ms + EPS) * g
    return h * (1.0 + scale) + shift


def _modnorm_kernel(x_ref, g_ref, sc_ref, sh_ref, o_ref):
    o_ref[...] = _modnorm(x_ref[...], g_ref[...], sc_ref[...], sh_ref[...]).astype(o_ref.dtype)


def _modnorm_call(x, g, scale, shift):
    B, T, D = x.shape
    bb, tt = _row_blocks(B, T, 512)
    return pl.pallas_call(
        _modnorm_kernel,
        grid=(B // bb, T // tt),
        in_specs=[pl.BlockSpec((bb, tt, D), lambda b, t: (b, t, 0)),
                  pl.BlockSpec((1, D), lambda b, t: (0, 0)),
                  pl.BlockSpec((bb, 1, D), lambda b, t: (b, 0, 0)),
                  pl.BlockSpec((bb, 1, D), lambda b, t: (b, 0, 0))],
        out_specs=pl.BlockSpec((bb, tt, D), lambda b, t: (b, t, 0)),
        out_shape=jax.ShapeDtypeStruct((B, T, D), BF16),
        compiler_params=_cparams(("parallel", "parallel")),
        name="modnorm",
    )(x, g.reshape(1, D), scale, shift)


def _mm_kernel(x_ref, w_ref, o_ref):
    bb, tt, K = x_ref.shape
    acc = jnp.dot(x_ref[...].reshape(bb * tt, K), w_ref[...], preferred_element_type=F32)
    o_ref[...] = acc.reshape(o_ref.shape).astype(o_ref.dtype)


def _mm_res_kernel(x_ref, w_ref, res_ref, gate_ref, o_ref):
    bb, tt, K = x_ref.shape
    acc = jnp.dot(x_ref[...].reshape(bb * tt, K), w_ref[...], preferred_element_type=F32)
    o_ref[...] = res_ref[...] + gate_ref[...] * acc.reshape(o_ref.shape)


def _mm(x, w, tn, out_dtype=F32, res=None, gate=None, name="mm"):
    B, T, K = x.shape
    N = w.shape[1]
    bb, tt = _row_blocks(B, T, 1024)
    in_specs = [pl.BlockSpec((bb, tt, K), lambda b, t, j: (b, t, 0)),
                pl.BlockSpec((K, tn), lambda b, t, j: (0, j))]
    args = [x, w]
    kern = _mm_kernel
    if res is not None:
        in_specs += [pl.BlockSpec((bb, tt, tn), lambda b, t, j: (b, t, j)),
                     pl.BlockSpec((bb, 1, tn), lambda b, t, j: (b, 0, j))]
        args += [res, gate]
        kern = _mm_res_kernel
    return pl.pallas_call(
        kern,
        grid=(B // bb, T // tt, N // tn),
        in_specs=in_specs,
        out_specs=pl.BlockSpec((bb, tt, tn), lambda b, t, j: (b, t, j)),
        out_shape=jax.ShapeDtypeStruct((B, T, N), out_dtype),
        compiler_params=_cparams(("parallel", "parallel", "arbitrary")),
        name=name,
    )(*args)


def _merge_kernel(hd_ref, ya_ref, yb_ref, yc_ref, yd_ref, wmu_ref, bmg_ref, wb_ref, o_ref):
    bb, tt, _ = hd_ref.shape
    rows = bb * tt
    hd = hd_ref[...].reshape(rows, MERGE_RANK).astype(BF16)
    acc = None
    for i, y_ref in enumerate((ya_ref, yb_ref, yc_ref, yd_ref)):
        gate = _sigmoid(jnp.dot(hd, wmu_ref[i], preferred_element_type=F32) + bmg_ref[i])
        term = gate * jnp.dot(y_ref[...].reshape(rows, BRANCH_W), wb_ref[i], preferred_element_type=F32)
        acc = term if acc is None else acc + term
    o_ref[...] = acc.reshape(o_ref.shape).astype(o_ref.dtype)


def _merge(p_small, ys, wmu, bmg, wb, tn=512):
    B, T, _ = p_small.shape
    D = wb.shape[2]
    bb, tt = _row_blocks(B, T, 512)
    yspec = pl.BlockSpec((bb, tt, BRANCH_W), lambda b, t, j: (b, t, 0))
    return pl.pallas_call(
        _merge_kernel,
        grid=(B // bb, T // tt, D // tn),
        in_specs=[pl.BlockSpec((bb, tt, MERGE_RANK), lambda b, t, j: (b, t, 1)),
                  yspec, yspec, yspec, yspec,
                  pl.BlockSpec((4, MERGE_RANK, tn), lambda b, t, j: (0, 0, j)),
                  pl.BlockSpec((4, 1, tn), lambda b, t, j: (0, 0, j)),
                  pl.BlockSpec((4, BRANCH_W, tn), lambda b, t, j: (0, 0, j))],
        out_specs=pl.BlockSpec((bb, tt, tn), lambda b, t, j: (b, t, j)),
        out_shape=jax.ShapeDtypeStruct((B, T, D), BF16),
        compiler_params=_cparams(("parallel", "parallel", "arbitrary")),
        name="merge",
    )(p_small, *ys, wmu, bmg, wb)


def _route_kernel(x_ref, g_ref, sc_ref, sh_ref, wh_ref, wl_ref, br_ref, h_ref, r_ref):
    bb, tt, D = x_ref.shape
    rows = bb * tt
    h = _modnorm(x_ref[...], g_ref[...], sc_ref[...], sh_ref[...])
    h_ref[...] = h.astype(h_ref.dtype)
    h = h.reshape(rows, D)
    hh, hl = _split2(h)
    f = lambda u, v: lax.dot_general(u, v, NT_DIMS, preferred_element_type=F32)
    lg = f(wh_ref[...], hh) + f(wh_ref[...], hl) + f(wl_ref[...], hh) + br_ref[...]
    gl = [lg[i:i + 1, :] for i in range(N_GROUPS)]
    el = [lg[N_GROUPS + i:N_GROUPS + i + 1, :] for i in range(N_EXPERTS)]
    best = gl[0]
    gidx = jnp.zeros(best.shape, jnp.int32)
    for i in range(1, N_GROUPS):
        m = gl[i] > best
        best = jnp.where(m, gl[i], best)
        gidx = jnp.where(m, i, gidx)
    den = None
    for i in range(N_GROUPS):
        e = jnp.exp(gl[i] - best)
        den = e if den is None else den + e
    p_grp = 1.0 / den
    e_in = []
    for j in range(EXP_PER_GROUP):
        v = el[j]
        for g in range(1, N_GROUPS):
            v = jnp.where(gidx == g, el[g * EXP_PER_GROUP + j], v)
        e_in.append(v)
    t1 = e_in[0]
    i1 = jnp.zeros(best.shape, jnp.int32)
    for j in range(1, EXP_PER_GROUP):
        m = e_in[j] > t1
        t1 = jnp.where(m, e_in[j], t1)
        i1 = jnp.where(m, j, i1)
    cand = [jnp.where(i1 == j, -jnp.inf, e_in[j]) for j in range(EXP_PER_GROUP)]
    t2 = cand[0]
    i2 = jnp.zeros(best.shape, jnp.int32)
    for j in range(1, EXP_PER_GROUP):
        m = cand[j] > t2
        t2 = jnp.where(m, cand[j], t2)
        i2 = jnp.where(m, j, i2)
    e2 = jnp.exp(t2 - t1)
    w1 = p_grp / (1.0 + e2)
    w2 = p_grp * e2 / (1.0 + e2)
    r_ref[0:1, :] = (gidx * EXP_PER_GROUP + i1).astype(F32)
    r_ref[1:2, :] = (gidx * EXP_PER_GROUP + i2).astype(F32)
    r_ref[2:3, :] = w1
    r_ref[3:4, :] = w2
    r_ref[4:8, :] = jnp.zeros((4, rows), F32)


def _route(x, g, scale, shift, wr_hi, wr_lo, br):
    B, T, D = x.shape
    bb, tt = _row_blocks(B, T, 512)
    nT = T // tt
    rows = bb * tt
    return pl.pallas_call(
        _route_kernel,
        grid=(B // bb, nT),
        in_specs=[pl.BlockSpec((bb, tt, D), lambda b, t: (b, t, 0)),
                  pl.BlockSpec((1, D), lambda b, t: (0, 0)),
                  pl.BlockSpec((bb, 1, D), lambda b, t: (b, 0, 0)),
                  pl.BlockSpec((bb, 1, D), lambda b, t: (b, 0, 0)),
                  pl.BlockSpec((128, D), lambda b, t: (0, 0)),
                  pl.BlockSpec((128, D), lambda b, t: (0, 0)),
                  pl.BlockSpec((128, 1), lambda b, t: (0, 0))],
        out_specs=[pl.BlockSpec((bb, tt, D), lambda b, t: (b, t, 0)),
                   pl.BlockSpec((8, rows), lambda b, t: (0, b * nT + t))],
        out_shape=[jax.ShapeDtypeStruct((B, T, D), BF16),
                   jax.ShapeDtypeStruct((8, B * T), F32)],
        compiler_params=_cparams(("parallel", "parallel")),
        name="route",
    )(x, g.reshape(1, D), scale, shift, wr_hi, wr_lo, br)


def _experts_kernel(te_ref, tv_ref, x_ref, wg_ref, wu_ref, wd_ref, o_ref):
    t = pl.program_id(0)
    f = pl.program_id(1)

    @pl.when(f == 0)
    def _():
        o_ref[...] = jnp.zeros(o_ref.shape, F32)

    @pl.when(tv_ref[t] > 0)
    def _():
        x = x_ref[...]
        hg = jnp.dot(x, wg_ref[...].astype(BF16), preferred_element_type=F32)
        hu = jnp.dot(x, wu_ref[...].astype(BF16), preferred_element_type=F32)
        hid = (_silu(hg) * hu).astype(BF16)
        o_ref[...] += jnp.dot(hid, wd_ref[...].astype(BF16), preferred_element_type=F32)


def _experts(tile_expert, tile_valid, xs, w_gate, w_up, w_down, layer, tm, fc=256):
    Mpad, D = xs.shape
    nf = D_FF // fc
    n_tiles = Mpad // tm
    w_gate = w_gate.reshape((-1,) + w_gate.shape[2:])
    w_up = w_up.reshape((-1,) + w_up.shape[2:])
    w_down = w_down.reshape((-1,) + w_down.shape[2:])
    e0 = layer * N_EXPERTS

    def w_in_map(t, f, te, tv):
        return (e0 + te[t], 0, jnp.where(tv[t] > 0, f, nf - 1))

    def w_out_map(t, f, te, tv):
        return (e0 + te[t], jnp.where(tv[t] > 0, f, nf - 1), 0)

    grid_spec = pltpu.PrefetchScalarGridSpec(
        num_scalar_prefetch=2,
        grid=(n_tiles, nf),
        in_specs=[pl.BlockSpec((tm, D), lambda t, f, te, tv: (t, 0)),
                  pl.BlockSpec((None, D, fc), w_in_map),
                  pl.BlockSpec((None, D, fc), w_in_map),
                  pl.BlockSpec((None, fc, D), w_out_map)],
        out_specs=pl.BlockSpec((tm, D), lambda t, f, te, tv: (t, 0)),
    )
    return pl.pallas_call(
        _experts_kernel,
        grid_spec=grid_spec,
        out_shape=jax.ShapeDtypeStruct((Mpad, D), F32),
        compiler_params=_cparams(("arbitrary", "arbitrary")),
        name="experts",
    )(tile_expert, tile_valid, xs, w_gate, w_up, w_down)


def _combine_kernel(x_ref, gate_ref, g0_ref, g1_ref, w0_ref, w1_ref, o_ref):
    o_ref[...] = x_ref[...] + gate_ref[...] * (w0_ref[...] * g0_ref[...] + w1_ref[...] * g1_ref[...])


def _combine_final_kernel(x_ref, gate_ref, g0_ref, g1_ref, w0_ref, w1_ref, gf_ref, o_ref):
    x = x_ref[...] + gate_ref[...] * (w0_ref[...] * g0_ref[...] + w1_ref[...] * g1_ref[...])
    ms = jnp.mean(x * x, axis=-1, keepdims=True)
    o_ref[...] = x * lax.rsqrt(ms + EPS) * gf_ref[...]


def _combine(x, gate, g0, g1, w0, w1, g_final=None):
    B, T, D = x.shape
    bb, tt = _row_blocks(B, T, 256)
    big = pl.BlockSpec((bb, tt, D), lambda b, t: (b, t, 0))
    col = pl.BlockSpec((bb, tt, 1), lambda b, t: (b, t, 0))
    in_specs = [big, pl.BlockSpec((bb, 1, D), lambda b, t: (b, 0, 0)), big, big, col, col]
    args = [x, gate, g0, g1, w0, w1]
    kern = _combine_kernel
    if g_final is not None:
        in_specs.append(pl.BlockSpec((1, D), lambda b, t: (0, 0)))
        args.append(g_final.reshape(1, D))
        kern = _combine_final_kernel
    return pl.pallas_call(
        kern,
        grid=(B // bb, T // tt),
        in_specs=in_specs,
        out_specs=big,
        out_shape=jax.ShapeDtypeStruct((B, T, D), F32),
        compiler_params=_cparams(("parallel", "parallel")),
        name="combine",
    )(*args)


def _moe(x1, gate2, g_norm2, scale2, shift2, wr_hi, wr_lo, br, w_e_gate, w_e_up, w_e_down, layer, g_final):
    B, T, D = x1.shape
    M = B * T
    h2, route = _route(x1, g_norm2, scale2, shift2, wr_hi, wr_lo, br)
    tm = 512 if M >= 4096 else 128
    eid = route[0:2].astype(jnp.int32).reshape(2 * M)
    wts = route[2:4]
    onehot = (eid[:, None] == jnp.arange(N_EXPERTS, dtype=jnp.int32)[None, :]).astype(jnp.int32)
    csum = jnp.cumsum(onehot, axis=0)
    counts = csum[-1]
    rank = jnp.sum((csum - onehot) * onehot, axis=1)
    padded = ((counts + tm - 1) // tm) * tm
    starts = jnp.cumsum(padded) - padded
    pos = starts[eid] + rank
    n_tiles = (2 * M) // tm + N_EXPERTS
    Mpad = n_tiles * tm
    token = jnp.tile(jnp.arange(M, dtype=jnp.int32), 2)
    token_of_pos = jnp.zeros((Mpad,), jnp.int32).at[pos].set(token)
    tile_start = jnp.arange(n_tiles, dtype=jnp.int32) * tm
    ends = starts + padded
    tile_expert = jnp.sum((tile_start[:, None] >= ends[None, :]).astype(jnp.int32), axis=1)
    tile_valid = (tile_expert < N_EXPERTS).astype(jnp.int32)
    last_e = jnp.max(jnp.where(counts > 0, jnp.arange(N_EXPERTS, dtype=jnp.int32), 0))
    tile_expert = jnp.where(tile_valid > 0, tile_expert, last_e).astype(jnp.int32)
    xs = h2.reshape(M, D).at[token_of_pos].get(mode="promise_in_bounds")
    out = _experts(tile_expert, tile_valid, xs, w_e_gate, w_e_up, w_e_down, layer, tm)
    g0 = out.at[pos[:M]].get(mode="promise_in_bounds").reshape(B, T, D)
    g1 = out.at[pos[M:]].get(mode="promise_in_bounds").reshape(B, T, D)
    w0 = wts[0].reshape(B, T, 1)
    w1 = wts[1].reshape(B, T, 1)
    return _combine(x1, gate2, g0, g1, w0, w1, g_final)


def _conv4(ext_ref, x, cw_ref, cb_ref, rows):
    ext_ref[8:8 + rows, :] = x
    out = cb_ref[...] + ext_ref[5:5 + rows, :] * cw_ref[0:1, :]
    out = out + ext_ref[6:6 + rows, :] * cw_ref[1:2, :]
    out = out + ext_ref[7:7 + rows, :] * cw_ref[2:3, :]
    return out + x * cw_ref[3:4, :]


def _lru_kernel(lx_ref, lg_ref, buf_ref, h0_ref, cw_ref, cb_ref, wa_ref, ba_ref, wx_ref, bx_ref, lam_ref,
                y_ref, nbuf_ref, hl_ref, ext_scr, a_scr, u_scr, h_scr, hc_scr):
    t = pl.program_id(1)
    Tc = lx_ref.shape[0]

    @pl.when(t == 0)
    def _():
        ext_scr[0:5, :] = jnp.zeros((5, BRANCH_W), F32)
        ext_scr[5:8, :] = buf_ref[...]
        hc_scr[...] = h0_ref[...]

    xc = _conv4(ext_scr, lx_ref[...], cw_ref, cb_ref, Tc)
    tail = ext_scr[Tc + 5:Tc + 8, :]
    ext_scr[5:8, :] = tail
    nbuf_ref[...] = tail

    def blockdiag(w_ref):
        return jnp.concatenate(
            [_dot(xc[:, n * LRU_BW:(n + 1) * LRU_BW], w_ref[n]) for n in range(LRU_BLOCKS)], axis=1)

    r = _sigmoid(blockdiag(wa_ref) + ba_ref[...])
    i = _sigmoid(blockdiag(wx_ref) + bx_ref[...])
    log_a = LRU_C * r * _log_sigmoid(lam_ref[...])
    a_scr[...] = jnp.exp(log_a)
    u_scr[...] = jnp.sqrt(-jnp.tanh(log_a) * (jnp.exp(2.0 * log_a) + 1.0)) * (i * xc)

    def body(s, h):
        h = a_scr[pl.ds(s, 1), :] * h + u_scr[pl.ds(s, 1), :]
        h_scr[pl.ds(s, 1), :] = h
        return h

    h = lax.fori_loop(0, Tc, body, hc_scr[...], unroll=8)
    hc_scr[...] = h
    hl_ref[...] = h
    y_ref[...] = (_gelu_tanh(lg_ref[...]) * h_scr[...]).astype(y_ref.dtype)


def _lru(p_lru, buf, h0, conv_w, conv_b, wa, ba, wx, bx, lam):
    B, T, _ = p_lru.shape
    W = BRANCH_W
    Tc = min(T, 256)
    row = lambda a: a.reshape(1, W)
    const = lambda shape: pl.BlockSpec(shape, lambda b, t: (0,) * len(shape))
    y, nbuf, hl = pl.pallas_call(
        _lru_kernel,
        grid=(B, T // Tc),
        in_specs=[pl.BlockSpec((None, Tc, W), lambda b, t: (b, t, 0)),
                  pl.BlockSpec((None, Tc, W), lambda b, t: (b, t, 1)),
                  pl.BlockSpec((None, CONV_W - 1, W), lambda b, t: (b, 0, 0)),
                  pl.BlockSpec((None, 1, W), lambda b, t: (b, 0, 0)),
                  const((CONV_W, W)), const((1, W)),
                  const((LRU_BLOCKS, LRU_BW, LRU_BW)), const((1, W)),
                  const((LRU_BLOCKS, LRU_BW, LRU_BW)), const((1, W)), const((1, W))],
        out_specs=[pl.BlockSpec((None, Tc, W), lambda b, t: (b, t, 0)),
                   pl.BlockSpec((None, CONV_W - 1, W), lambda b, t: (b, 0, 0)),
                   pl.BlockSpec((None, 1, W), lambda b, t: (b, 0, 0))],
        out_shape=[jax.ShapeDtypeStruct((B, T, W), BF16),
                   jax.ShapeDtypeStruct((B, CONV_W - 1, W), F32),
                   jax.ShapeDtypeStruct((B, 1, W), F32)],
        scratch_shapes=[pltpu.VMEM((Tc + 8, W), F32), pltpu.VMEM((Tc, W), F32), pltpu.VMEM((Tc, W), F32),
                        pltpu.VMEM((Tc, W), F32), pltpu.VMEM((1, W), F32)],
        compiler_params=_cparams(("parallel", "arbitrary")),
        name="rglru",
    )(p_lru, p_lru, buf, h0.reshape(B, 1, W), conv_w, row(conv_b), wa, row(ba), wx, row(bx), row(lam))
    return y, nbuf, hl.reshape(B, W)


def _gla_kernel(q_ref, k_ref, v_ref, g_ref, ga_ref, s0_ref, wup_ref, wupt_ref, b_ref, bt_ref, gn_ref,
                y_ref, s_ref, s_scr):
    t = pl.program_id(1)
    nT = pl.num_programs(1)
    Tv = q_ref.shape[0]
    C = max(Tv, CHUNK)

    @pl.when(t == 0)
    def _():
        s_scr[...] = s0_ref[...]

    q = _pad_rows(q_ref[...], C) * (GLA_DK ** -0.5)
    k = _pad_rows(k_ref[...], C)
    v = _pad_rows(v_ref[...], C)
    ga = _pad_rows(ga_ref[...], C)
    la =_log_sigmoid(_dot_hl(ga, wup_ref[...]) + b_ref[...]) / GLA_NORMALIZER
    lat = _log_sigmoid(_dot_hl_nt(wupt_ref[...], ga) + bt_ref[...]) / GLA_NORMALIZER
    if Tv < C:
        la = jnp.where(lax.broadcasted_iota(jnp.int32, la.shape, 0) < Tv, la, 0.0)
        lat = jnp.where(lax.broadcasted_iota(jnp.int32, lat.shape, 1) < Tv, lat, 0.0)
    tri = jnp.where(_tri(C), 1.0, 0.0).astype(BF16)
    bc = _dot_3x(tri, la)
    b_last = bc[C - 1:C, :]
    b_mid = bc[C // 2:C // 2 + 1, :]
    b_last_col = jnp.sum(lat, axis=1, keepdims=True)
    q_in = q * jnp.exp(bc)
    q_att = q * jnp.exp(bc - b_mid)
    k_att = k * jnp.exp(b_mid - bc)
    k_dec = k * jnp.exp(b_last - bc)
    causal = _tri(C)
    gn = gn_ref[...]
    outs = []
    for h in range(GLA_HEADS):
        ks = slice(h * GLA_DK, (h + 1) * GLA_DK)
        vs = slice(h * GLA_DV, (h + 1) * GLA_DV)
        vh = v[:, vs]
        S = s_scr[h]
        att = jnp.where(causal, _dot_nt(q_att[:, ks], k_att[:, ks]), 0.0)
        o = _dot(att, vh) + _dot(q_in[:, ks], S)
        s_scr[h] = S * jnp.exp(b_last_col[ks, :]) + _dot_tn(k_dec[:, ks], vh)
        o = o * lax.rsqrt(jnp.mean(o * o, axis=-1, keepdims=True) + EPS) * gn
        outs.append(o)
    o = jnp.concatenate(outs, axis=1)[0:Tv]
    y_ref[...] = (o * _silu(g_ref[...])).astype(y_ref.dtype)

    @pl.when(t == nT - 1)
    def _():
        s_ref[...] = s_scr[...]


def _gla(p_gla, p_small, S0, w_up, b_a, g_norm):
    B, T, _ = p_gla.shape
    Tv = min(T, CHUNK)
    const = lambda shape: pl.BlockSpec(shape, lambda b, t: (0,) * len(shape))
    st = (GLA_HEADS, GLA_DK, GLA_DV)
    w_up_pad = jnp.concatenate([w_up, jnp.zeros((128 - GLA_RANK, GLA_QK), F32)], axis=0)
    y, S = pl.pallas_call(
        _gla_kernel,
        grid=(B, T // Tv),
        in_specs=[pl.BlockSpec((None, Tv, GLA_QK), lambda b, t: (b, t, 0)),
                  pl.BlockSpec((None, Tv, GLA_QK), lambda b, t: (b, t, 1)),
                  pl.BlockSpec((None, Tv, BRANCH_W), lambda b, t: (b, t, 1)),
                  pl.BlockSpec((None, Tv, BRANCH_W), lambda b, t: (b, t, 2)),
                  pl.BlockSpec((None, Tv, 128), lambda b, t: (b, t, 0)),
                  pl.BlockSpec((None,) + st, lambda b, t: (b, 0, 0, 0)),
                  const((128, GLA_QK)), const((GLA_QK, 128)),
                  const((1, GLA_QK)), const((GLA_QK, 1)), const((1, GLA_DV))],
        out_specs=[pl.BlockSpec((None, Tv, BRANCH_W), lambda b, t: (b, t, 0)),
                   pl.BlockSpec((None,) + st, lambda b, t: (b, 0, 0, 0))],
        out_shape=[jax.ShapeDtypeStruct((B, T, BRANCH_W), BF16),
                   jax.ShapeDtypeStruct((B,) + st, F32)],
        scratch_shapes=[pltpu.VMEM(st, F32)],
        compiler_params=_cparams(("parallel", "arbitrary")),
        name="gla",
    )(p_gla, p_gla, p_gla, p_gla, p_small, S0, w_up_pad, w_up_pad.T, b_a.reshape(1, GLA_QK),
      b_a.reshape(GLA_QK, 1), g_norm.reshape(1, GLA_DV))
    return y, S


def _mlstm_kernel(qk_ref, v_ref, o_ref, if_ref, ift_ref, buf_ref, c0_ref, n0_ref, m0_ref,
                  cw_ref, cb_ref, bif_ref, bift_ref, ng_ref,
                  y_ref, nbuf_ref, c_ref, n_ref, m_ref, ext_scr, c_scr, n_scr, m_scr):
    t = pl.program_id(1)
    nT = pl.num_programs(1)
    Tv = qk_ref.shape[0]
    C = max(Tv, CHUNK)
    W2 = 2 * BRANCH_W

    @pl.when(t == 0)
    def _():
        ext_scr[0:5, :] = jnp.zeros((5, W2), F32)
        ext_scr[5:8, :] = buf_ref[...]
        c_scr[...] = c0_ref[...]
        n_scr[...] = n0_ref[...]
        m_scr[...] = jnp.zeros(m_scr.shape, F32)
        m_scr[0:1, 0:ML_HEADS] = m0_ref[...]

    qk = _conv4(ext_scr, qk_ref[...], cw_ref, cb_ref, Tv)
    tail = ext_scr[Tv + 5:Tv + 8, :]
    ext_scr[5:8, :] = tail
    nbuf_ref[...] = tail
    qk = _pad_rows(_silu(qk), C)
    q = qk[:, 0:BRANCH_W]
    k = qk[:, BRANCH_W:W2] * (ML_DH ** -0.5)
    v = _pad_rows(v_ref[...], C)

    gates = _pad_rows(if_ref[...], C) + bif_ref[...]
    i_col = gates
    lf_col = _log_sigmoid(gates)
    gates_t = _pad_rows(ift_ref[...] + bift_ref[...], 16)
    if Tv < C:
        gates_t = jnp.concatenate([gates_t, jnp.zeros((16, C - Tv), F32)], axis=1)
    i_row = gates_t
    lf_row = _log_sigmoid(gates_t)
    if Tv < C:
        rmask = lax.broadcasted_iota(jnp.int32, (C, 128), 0) < Tv
        cmask = lax.broadcasted_iota(jnp.int32, (16, C), 1) < Tv
        i_col = jnp.where(rmask, i_col, NEG_BIG)
        lf_col = jnp.where(rmask, lf_col, 0.0)
        i_row = jnp.where(cmask, i_row, NEG_BIG)
        lf_row = jnp.where(cmask, lf_row, 0.0)
    causal = _tri(C)
    tri = jnp.where(causal, 1.0, 0.0).astype(BF16)
    triu = jnp.where(_tri(C, upper=True), 1.0, 0.0).astype(BF16)
    F_col = _dot_3x(tri, lf_col)
    F_row = _dot_x3(lf_row, triu)
    ng = ng_ref[...]
    outs = []
    for h in range(ML_HEADS):
        hs = slice(h * ML_DH, (h + 1) * ML_DH)
        qh, kh, vh = q[:, hs], k[:, hs], v[:, hs]
        Fc = F_col[:, 16 + ML_HEADS + h:17 + ML_HEADS + h]
        ic = i_col[:, 16 + h:17 + h]
        m_prev = m_scr[0:1, h:h + 1]
        Fr = F_row[ML_HEADS + h:ML_HEADS + h + 1, :]
        log_d = jnp.where(causal, Fc - Fr + i_row[h:h + 1, :], NEG_BIG)
        log_inter = Fc + m_prev
        m_t = jnp.maximum(log_inter, jnp.max(log_d, axis=1, keepdims=True))
        s = _dot_nt(qh, kh) * jnp.exp(log_d - m_t)
        inter = jnp.exp(log_inter - m_t)
        Ch = c_scr[h]
        nh = n_scr[h:h + 1, :]
        num = _dot(s, vh) + inter * _dot(qh, Ch)
        den = jnp.sum(s, axis=1, keepdims=True) + inter * jnp.sum(qh * nh, axis=1, keepdims=True)
        hh = num / jnp.maximum(jnp.abs(den), jnp.exp(-m_t))
        m_new = m_t[C - 1:C, :]
        F_last = Fc[C - 1:C, :]
        decay = jnp.exp(F_last + m_prev - m_new)
        w_s = jnp.exp(F_last - Fc + ic - m_new)
        kw = kh * w_s
        c_scr[h] = decay * Ch + _dot_tn(kw, vh)
        n_scr[h:h + 1, :] = decay * nh + jnp.sum(kw, axis=0, keepdims=True)
        m_scr[0:1, h:h + 1] = m_new
        hh = hh * lax.rsqrt(jnp.mean(hh * hh, axis=-1, keepdims=True) + EPS) * ng[:, hs]
        outs.append(hh)
    hcat = jnp.concatenate(outs, axis=1)[0:Tv]
    y_ref[...] = (hcat * _sigmoid(o_ref[...])).astype(y_ref.dtype)

    @pl.when(t == nT - 1)
    def _():
        c_ref[...] = c_scr[...]
        n_ref[...] = n_scr[...]
        m_ref[...] = m_scr[0:1, 0:ML_HEADS]


def _mlstm(p_ml, p_small, buf, C0, n0, m0, conv_w, conv_b, b_if, norm_g):
    B, T, _ = p_ml.shape
    W = BRANCH_W
    Tv = min(T, CHUNK)
    ift = jnp.swapaxes(p_small[:, :, 16:16 + 2 * ML_HEADS], 1, 2)
    const = lambda shape: pl.BlockSpec(shape, lambda b, t: (0,) * len(shape))
    cst = (ML_HEADS, ML_DH, ML_DH)
    y, nbuf, Cn, nn, mn = pl.pallas_call(
        _mlstm_kernel,
        grid=(B, T // Tv),
        in_specs=[pl.BlockSpec((None, Tv, 2 * W), lambda b, t: (b, t, 0)),
                  pl.BlockSpec((None, Tv, W), lambda b, t: (b, t, 2)),
                  pl.BlockSpec((None, Tv, W), lambda b, t: (b, t, 3)),
                  pl.BlockSpec((None, Tv, 128), lambda b, t: (b, t, 0)),
                  pl.BlockSpec((None, 2 * ML_HEADS, Tv), lambda b, t: (b, 0, t)),
                  pl.BlockSpec((None, CONV_W - 1, 2 * W), lambda b, t: (b, 0, 0)),
                  pl.BlockSpec((None,) + cst, lambda b, t: (b, 0, 0, 0)),
                  pl.BlockSpec((None, ML_HEADS, ML_DH), lambda b, t: (b, 0, 0)),
                  pl.BlockSpec((None, 1, ML_HEADS), lambda b, t: (b, 0, 0)),
                  const((CONV_W, 2 * W)), const((1, 2 * W)),
                  const((1, 128)), const((2 * ML_HEADS, 1)), const((1, W))],
        out_specs=[pl.BlockSpec((None, Tv, W), lambda b, t: (b, t, 0)),
                   pl.BlockSpec((None, CONV_W - 1, 2 * W), lambda b, t: (b, 0, 0)),
                   pl.BlockSpec((None,) + cst, lambda b, t: (b, 0, 0, 0)),
                   pl.BlockSpec((None, ML_HEADS, ML_DH), lambda b, t: (b, 0, 0)),
                   pl.BlockSpec((None, 1, ML_HEADS), lambda b, t: (b, 0, 0))],
        out_shape=[jax.ShapeDtypeStruct((B, T, W), BF16),
                   jax.ShapeDtypeStruct((B, CONV_W - 1, 2 * W), F32),
                   jax.ShapeDtypeStruct((B,) + cst, F32),
                   jax.ShapeDtypeStruct((B, ML_HEADS, ML_DH), F32),
                   jax.ShapeDtypeStruct((B, 1, ML_HEADS), F32)],
        scratch_shapes=[pltpu.VMEM((Tv + 8, 2 * W), F32), pltpu.VMEM(cst, F32),
                        pltpu.VMEM((ML_HEADS, ML_DH), F32), pltpu.VMEM((8, 128), F32)],
        compiler_params=_cparams(("parallel", "arbitrary")),
        name="mlstm",
    )(p_ml, p_ml, p_ml, p_small, ift, buf, C0, n0, m0.reshape(B, 1, ML_HEADS),
      conv_w, conv_b.reshape(1, 2 * W),
      jnp.zeros((1, 128), F32).at[0, 16:16 + 2 * ML_HEADS].set(b_if), b_if.reshape(2 * ML_HEADS, 1),
      norm_g.reshape(1, W))
    return y, nbuf, Cn, nn, mn.reshape(B, ML_HEADS)


def _rwkv_kernel(p_ref, sh_ref, s0_ref, mu_ref, w0_ref, wup_ref, a0_ref, aup_ref, gup_ref,
                 kk_ref, ka_ref, rk_ref, lng_ref, lnb_ref,
                 y_ref, shn_ref, s_ref,
                 ext_scr, nkk_scr, w_scr, kka_scr, k2_scr, x_scr, v_scr, g_scr, bon_scr, yv_scr,
                 st_scr, yr_scr):
    t = pl.program_id(1)
    nT = pl.num_programs(1)
    Tb = p_ref.shape[0]
    W = BRANCH_W

    @pl.when(t == 0)
    def _():
        ext_scr[0:7, :] = jnp.zeros((7, RW_NCOLS), F32)
        ext_scr[7:8, :] = sh_ref[...]
        st_scr[...] = s0_ref[...]

    ones_blk = _seg_ones(256, RW_DH)
    p = p_ref[...]
    ext_scr[8:8 + Tb, :] = p
    prev = ext_scr[7:7 + Tb, :]
    last = p[Tb - 1:Tb, :]
    ext_scr[7:8, :] = last
    shn_ref[...] = last
    pm = p + (prev - p) * mu_ref[...]
    r = pm[:, 0:W]
    k = pm[:, W:2 * W]
    v = pm[:, 2 * W:3 * W]
    wa_d = pm[:, 3 * W:3 * W + 128]
    gd = pm[:, 3 * W + 128:3 * W + 256]
    log_w = -RW_DECAY_SCALE * _sigmoid(w0_ref[...] + _dot(jnp.tanh(wa_d), wup_ref[...]))
    a = _sigmoid(a0_ref[...] + _dot(wa_d, aup_ref[...]))
    g_scr[...] = _dot(_sigmoid(gd), gup_ref[...])
    kkr = k * kk_ref[...]
    kk = kkr / jnp.maximum(jnp.sqrt(_segsum3(kkr * kkr, ones_blk)), 1e-12)
    k2 = k * (1.0 + (a - 1.0) * ka_ref[...])
    bon_scr[...] = _segsum3(r * k2 * rk_ref[...], ones_blk) * v
    w = jnp.exp(log_w)
    kka = kk * a
    c1 = _segsum3(kka * r, ones_blk)
    yv_scr[...] = _segsum3(k2 * r, ones_blk) * v
    x_scr[...] = w * r - kk * c1
    nkk_scr[...] = -kk
    w_scr[...] = w
    kka_scr[...] = kka
    k2_scr[...] = k2
    v_scr[...] = v

    BL = 256
    eye_t = (lax.broadcasted_iota(jnp.int32, (RW_DH, BL), 0)
             == lax.broadcasted_iota(jnp.int32, (RW_DH, BL), 1) % RW_DH)
    eye_f = jnp.where(eye_t, 1.0, 0.0)
    NG = W // BL

    def step(row, c):
        r1 = pl.ds(row, 1)
        S = [st_scr[:, pl.ds(gi * BL, BL)] for gi in range(NG)]
        for half in range(2):
            gis = list(range(half * NG // 2, (half + 1) * NG // 2))
            lss = [pl.ds(gi * BL, BL) for gi in gis]
            lhs = ([S[gi] * nkk_scr[r1, ls] for gi, ls in zip(gis, lss)]
                   + [eye_f * v_scr[r1, ls] for ls in lss]
                   + [S[gi] * x_scr[r1, ls] for gi, ls in zip(gis, lss)])
            res = jnp.dot(jnp.concatenate(lhs, axis=0).astype(BF16), ones_blk, preferred_element_type=F32)
            n = len(gis)
            for j, gi in enumerate(gis):
                ls = lss[j]
                sa = res[j * RW_DH:(j + 1) * RW_DH]
                vc = res[(n + j) * RW_DH:(n + j + 1) * RW_DH]
                yq = res[(2 * n + j) * RW_DH:(2 * n + j + 1) * RW_DH]
                st_scr[:, ls] = S[gi] * w_scr[r1, ls] + sa * kka_scr[r1, ls] + vc * k2_scr[r1, ls]
                yr_scr[r1, ls] = jnp.sum(yq * eye_f, axis=0, keepdims=True)
        return c

    lax.fori_loop(0, Tb, step, 0, unroll=4)

    y = yr_scr[...] + yv_scr[...]
    mean = _segsum3(y, ones_blk) * (1.0 / RW_DH)
    dlt = y - mean
    var = _segsum3(dlt * dlt, ones_blk) * (1.0 / RW_DH)
    yn = dlt * lax.rsqrt(var + RW_LN_EPS) * lng_ref[...] + lnb_ref[...]
    y_ref[...] = ((yn + bon_scr[...]) * g_scr[...]).astype(y_ref.dtype)

    @pl.when(t == nT - 1)
    def _():
        s_ref[...] = st_scr[...]


def _rwkv(p_rw, shift_prev, S0, mu, w0, w_up, a0, a_up, g_up, k_k, k_a, r_k, ln_g, ln_b):
    B, T, _ = p_rw.shape
    W = BRANCH_W
    Tb = min(T, 256)
    s0 = jnp.transpose(S0, (0, 2, 1, 3)).reshape(B, RW_DH, W)
    row = lambda a: a.reshape(1, -1)
    const = lambda shape: pl.BlockSpec(shape, lambda b, t: (0,) * len(shape))
    rows = lambda: pltpu.VMEM((Tb, W), F32)
    y, shn, S = pl.pallas_call(
        _rwkv_kernel,
        grid=(B, T // Tb),
        in_specs=[pl.BlockSpec((None, Tb, RW_NCOLS), lambda b, t: (b, t, 0)),
                  pl.BlockSpec((None, 1, RW_NCOLS), lambda b, t: (b, 0, 0)),
                  pl.BlockSpec((None, RW_DH, W), lambda b, t: (b, 0, 0)),
                  const((1, RW_NCOLS)), const((1, W)), const((128, W)), const((1, W)), const((128, W)),
                  const((128, W)), const((1, W)), const((1, W)), const((1, W)), const((1, W)), const((1, W))],
        out_specs=[pl.BlockSpec((None, Tb, W), lambda b, t: (b, t, 0)),
                   pl.BlockSpec((None, 1, RW_NCOLS), lambda b, t: (b, 0, 0)),
                   pl.BlockSpec((None, RW_DH, W), lambda b, t: (b, 0, 0))],
        out_shape=[jax.ShapeDtypeStruct((B, T, W), BF16),
                   jax.ShapeDtypeStruct((B, 1, RW_NCOLS), F32),
                   jax.ShapeDtypeStruct((B, RW_DH, W), F32)],
        scratch_shapes=[pltpu.VMEM((Tb + 8, RW_NCOLS), F32),
                        rows(), rows(), rows(), rows(), rows(), rows(), rows(), rows(), rows(),
                        pltpu.VMEM((RW_DH, W), F32), rows()],
        compiler_params=_cparams(("parallel", "arbitrary")),
        name="rwkv7",
    )(p_rw, shift_prev.reshape(B, 1, RW_NCOLS), s0, row(mu), row(w0),
      jnp.concatenate([w_up, jnp.zeros_like(w_up)], axis=0), row(a0),
      jnp.concatenate([jnp.zeros_like(a_up), a_up], axis=0), g_up,
      row(k_k), row(k_a), row(r_k), row(ln_g), row(ln_b))
    S = jnp.transpose(S.reshape(B, RW_DH, RW_HEADS, RW_DH), (0, 2, 1, 3))
    return y, shn.reshape(B, RW_NCOLS), S


def _prep_layer(l, w_in, w_mg_down, w_mg_up, b_mg, w_branch, w_out, w_route_g, b_route_g, w_route_e, b_route_e):
    W = BRANCH_W
    wi = w_in[l]
    offs = [0]
    for s in (GLA_QK, GLA_QK, W, W, GLA_RANK, RW_NCOLS, 2 * W, W, W, 2 * ML_HEADS, W, W):
        offs.append(offs[-1] + s)
    seg = lambda i: wi[:, offs[i]:offs[i + 1]]
    w_gla = jnp.concatenate([seg(0), seg(1), seg(2), seg(3)], axis=1).astype(BF16)
    w_rw = seg(5).astype(BF16)
    w_ml = jnp.concatenate([seg(6), seg(7), seg(8)], axis=1).astype(BF16)
    w_lru = jnp.concatenate([seg(10), seg(11)], axis=1).astype(BF16)
    pad = jnp.zeros((D_MODEL, 256 - GLA_RANK - 2 * ML_HEADS), F32)
    w_small = jnp.concatenate([seg(4), seg(9), pad, w_mg_down[l]], axis=1).astype(BF16)
    wmu = jnp.transpose(w_mg_up[l].reshape(MERGE_RANK, 4, D_MODEL), (1, 0, 2)).astype(BF16)
    bmg = b_mg[l].reshape(4, 1, D_MODEL)
    wr = jnp.concatenate([w_route_g[l], w_route_e[l]], axis=1).T
    wr = jnp.concatenate([wr, jnp.zeros((128 - wr.shape[0], D_MODEL), F32)], axis=0)
    wr_hi = wr.astype(BF16)
    wr_lo = (wr - wr_hi.astype(F32)).astype(BF16)
    br = jnp.concatenate([b_route_g[l], b_route_e[l], jnp.zeros((128 - 20,), F32)]).reshape(128, 1)
    return dict(w_gla=w_gla, w_rw=w_rw, w_ml=w_ml, w_lru=w_lru, w_small=w_small, wmu=wmu, bmg=bmg,
                wb=w_branch[l].astype(BF16), w_out=w_out[l].astype(BF16), wr_hi=wr_hi, wr_lo=wr_lo, br=br)


def _run_trunk(x, mod, states, lw, prep, g_final):
    n_layers = mod.shape[0]
    new_states = []
    for l in range(n_layers):
        pw = prep[l]
        st = [s[l] for s in states]
        gla_S, rw_S, rw_shift, ml_C, ml_n, ml_m, ml_conv, lru_h, lru_conv = st
        m = mod[l]
        shift1, scale1, gate1, shift2, scale2, gate2 = [m[:, i:i + 1, :] for i in range(6)]
        h = _modnorm_call(x, lw["g_norm1"][l], scale1, shift1)
        p_gla = _mm(h, pw["w_gla"], 512, name="proj_gla")
        p_rw = _mm(h, pw["w_rw"], 256, name="proj_rw")
        p_ml = _mm(h, pw["w_ml"], 512, name="proj_ml")
        p_lru = _mm(h, pw["w_lru"], 512, name="proj_lru")
        p_small = _mm(h, pw["w_small"], 512, name="proj_small")
        y_a, gla_S = _gla(p_gla, p_small, gla_S, lw["gla_w_up"][l], lw["gla_b"][l], lw["gla_g_norm"][l])
        y_b, rw_shift, rw_S = _rwkv(p_rw, rw_shift, rw_S, lw["rw_mu"][l], lw["rw_w0"][l], lw["rw_w_up"][l],
                                    lw["rw_a0"][l], lw["rw_a_up"][l], lw["rw_g_up"][l], lw["rw_k_k"][l],
                                    lw["rw_k_a"][l], lw["rw_r_k"][l], lw["rw_ln_g"][l], lw["rw_ln_b"][l])
        y_c, ml_conv, ml_C, ml_n, ml_m = _mlstm(p_ml, p_small, ml_conv, ml_C, ml_n, ml_m, lw["ml_conv_w"][l],
                                                lw["ml_conv_b"][l], lw["ml_b_if"][l], lw["ml_norm_g"][l])
        y_d, lru_conv, lru_h = _lru(p_lru, lru_conv, lru_h, lw["lru_conv_w"][l], lw["lru_conv_b"][l],
                                    lw["lru_wa"][l], lw["lru_ba"][l], lw["lru_wx"][l], lw["lru_bx"][l],
                                    lw["lru_lambda"][l])
        merged = _merge(p_small, (y_a, y_b, y_c, y_d), pw["wmu"], pw["bmg"], pw["wb"])
        x1 = _mm(merged, pw["w_out"], 512, res=x, gate=gate1, name="out_proj")
        x = _moe(x1, gate2, lw["g_norm2"][l], scale2, shift2, pw["wr_hi"], pw["wr_lo"], pw["br"],
                 lw["w_e_gate"], lw["w_e_up"], lw["w_e_down"], l,
                 g_final if l == n_layers - 1 else None)
        new_states.append((gla_S, rw_S, rw_shift, ml_C, ml_n, ml_m, ml_conv, lru_h, lru_conv))
    return x, [jnp.stack([st[i] for st in new_states]) for i in range(9)]


def kernel(x_prompt, x_sample, c_prompt, c_sample, state_gla_S, state_rwkv_S, state_rwkv_shift, state_mlstm_C, state_mlstm_n, state_mlstm_m, state_mlstm_conv, state_lru_h, state_lru_conv, w_ada, b_ada, g_norm1, g_norm2, w_in, gla_w_up, gla_b, gla_g_norm, rw_mu, rw_w0, rw_w_up, rw_a0, rw_a_up, rw_g_up, rw_k_k, rw_k_a, rw_r_k, rw_ln_g, rw_ln_b, ml_conv_w, ml_conv_b, ml_b_if, ml_norm_g, lru_conv_w, lru_conv_b, lru_wa, lru_ba, lru_wx, lru_bx, lru_lambda, w_branch, w_mg_down, w_mg_up, b_mg, w_out, w_route_g, b_route_g, w_route_e, b_route_e, w_e_gate, w_e_up, w_e_down, g_final):
    n_layers = w_ada.shape[0]
    Bp, Bs = x_prompt.shape[0], x_sample.shape[0]
    lw = dict(g_norm1=g_norm1, g_norm2=g_norm2, gla_w_up=gla_w_up, gla_b=gla_b, gla_g_norm=gla_g_norm,
              rw_mu=rw_mu, rw_w0=rw_w0, rw_w_up=rw_w_up, rw_a0=rw_a0, rw_a_up=rw_a_up, rw_g_up=rw_g_up,
              rw_k_k=rw_k_k, rw_k_a=rw_k_a, rw_r_k=rw_r_k, rw_ln_g=rw_ln_g, rw_ln_b=rw_ln_b,
              ml_conv_w=ml_conv_w, ml_conv_b=ml_conv_b, ml_b_if=ml_b_if, ml_norm_g=ml_norm_g,
              lru_conv_w=lru_conv_w, lru_conv_b=lru_conv_b, lru_wa=lru_wa, lru_ba=lru_ba, lru_wx=lru_wx,
              lru_bx=lru_bx, lru_lambda=lru_lambda, w_e_gate=w_e_gate, w_e_up=w_e_up, w_e_down=w_e_down)
    prep = [_prep_layer(l, w_in, w_mg_down, w_mg_up, b_mg, w_branch, w_out,
                        w_route_g, b_route_g, w_route_e, b_route_e) for l in range(n_layers)]
    nb = Bp + Bs
    rows = ((nb + 15) // 16) * 16
    c_all = jnp.concatenate([c_prompt, c_sample, jnp.zeros((rows - nb, D_MODEL), F32)], axis=0)
    mod = _ada_mod(c_all, w_ada, b_ada).reshape(n_layers, rows, 6, D_MODEL)
    mod_p = mod[:, 0:Bp]
    mod_s = mod[:, Bp:nb]

    def zeros(*s):
        return jnp.zeros((n_layers, Bp) + s, F32)

    zero_states = (zeros(GLA_HEADS, GLA_DK, GLA_DV), zeros(RW_HEADS, RW_DH, RW_DH), zeros(RW_NCOLS),
                   zeros(ML_HEADS, ML_DH, ML_DH), zeros(ML_HEADS, ML_DH), zeros(ML_HEADS),
                   zeros(CONV_W - 1, 2 * BRANCH_W), zeros(BRANCH_W), zeros(CONV_W - 1, BRANCH_W))
    sample_states = (state_gla_S, state_rwkv_S, state_rwkv_shift, state_mlstm_C, state_mlstm_n,
                     state_mlstm_m, state_mlstm_conv, state_lru_h, state_lru_conv)
    y_prompt, ps = _run_trunk(x_prompt, mod_p, zero_states, lw, prep, g_final)
    y_sample, ss = _run_trunk(x_sample, mod_s, sample_states, lw, prep, g_final)
    return (y_prompt, y_sample, *ps, *ss)
```

```python
import functools

import jax
import jax.numpy as jnp
from jax import lax
from jax.experimental import pallas as pl
from jax.experimental.pallas import tpu as pltpu

F32 = jnp.float32
BF16 = jnp.bfloat16

D_MODEL = 4096
BRANCH_W = D_MODEL // 4
EPS = 1e-6
GLA_HEADS = 4
GLA_DK = 128
GLA_DV = 256
GLA_QK = GLA_HEADS * GLA_DK
GLA_RANK = 16
GLA_NORMALIZER = 16.0
RW_DH = 64
RW_HEADS = BRANCH_W // RW_DH
RW_NCOLS = 3 * BRANCH_W + 256
RW_DECAY_SCALE = 0.606531
RW_LN_EPS = 64e-5
ML_HEADS = 4
ML_DH = BRANCH_W // ML_HEADS
CONV_W = 4
LRU_BLOCKS = 8
LRU_BW = BRANCH_W // LRU_BLOCKS
LRU_C = 8.0
N_GROUPS = 4
EXP_PER_GROUP = 4
N_EXPERTS = 16
D_FF = D_MODEL // 4
MERGE_RANK = 256

VMEM_LIMIT = 56 * 1024 * 1024
CHUNK = 128
NEG_BIG = -1e30

NT_DIMS = (((1,), (1,)), ((), ()))
TN_DIMS = (((0,), (0,)), ((), ()))


def _cparams(sem):
    return pltpu.CompilerParams(dimension_semantics=sem, vmem_limit_bytes=VMEM_LIMIT)


def _dot(a, b):
    return jnp.dot(a.astype(BF16), b.astype(BF16), preferred_element_type=F32)


def _dot_nt(a, b):
    return lax.dot_general(a.astype(BF16), b.astype(BF16), NT_DIMS, preferred_element_type=F32)


def _dot_tn(a, b):
    return lax.dot_general(a.astype(BF16), b.astype(BF16), TN_DIMS, preferred_element_type=F32)


def _split3(x):
    hi = x.astype(BF16)
    r1 = x - hi.astype(F32)
    mid = r1.astype(BF16)
    lo = (r1 - mid.astype(F32)).astype(BF16)
    return hi, mid, lo


def _split2(x):
    hi = x.astype(BF16)
    lo = (x - hi.astype(F32)).astype(BF16)
    return hi, lo


def _dot_x3(a, b_exact):
    hi, mid, lo = _split3(a)
    f = lambda u: jnp.dot(u, b_exact, preferred_element_type=F32)
    return f(hi) + f(mid) + f(lo)


def _dot_3x(a_exact, b):
    hi, mid, lo = _split3(b)
    f = lambda u: jnp.dot(a_exact, u, preferred_element_type=F32)
    return f(hi) + f(mid) + f(lo)


def _dot_hl(a, b):
    ah, al = _split2(a)
    bh, bl = _split2(b)
    f = lambda u, v: jnp.dot(u, v, preferred_element_type=F32)
    return f(ah, bh) + f(al, bh) + f(ah, bl)


def _dot_hl_nt(a, b):
    ah, al = _split2(a)
    bh, bl = _split2(b)
    f = lambda u, v: lax.dot_general(u, v, NT_DIMS, preferred_element_type=F32)
    return f(ah, bh) + f(al, bh) + f(ah, bl)


def _sigmoid(x):
    return 1.0 / (1.0 + jnp.exp(-x))


def _silu(x):
    return x * _sigmoid(x)


def _log_sigmoid(x):
    return jnp.minimum(x, 0.0) - jnp.log(1.0 + jnp.exp(-jnp.abs(x)))


def _gelu_tanh(x):
    return 0.5 * x * (1.0 + jnp.tanh(0.7978845608028654 * (x + 0.044715 * x * x * x)))


def _tri(n, upper=False):
    r = lax.broadcasted_iota(jnp.int32, (n, n), 0)
    c = lax.broadcasted_iota(jnp.int32, (n, n), 1)
    m = (r <= c) if upper else (r >= c)
    return m


def _seg_ones(width, seg):
    r = lax.broadcasted_iota(jnp.int32, (width, width), 0) // seg
    c = lax.broadcasted_iota(jnp.int32, (width, width), 1) // seg
    return jnp.where(r == c, 1.0, 0.0).astype(BF16)


def _segsum_bf16(x_bf16, ones_blk):
    bw = ones_blk.shape[0]
    n = x_bf16.shape[1] // bw
    return jnp.concatenate(
        [jnp.dot(x_bf16[:, g * bw:(g + 1) * bw], ones_blk, preferred_element_type=F32) for g in range(n)],
        axis=1)


def _segsum3(x, ones_blk):
    hi, mid, lo = _split3(x)
    return _segsum_bf16(hi, ones_blk) + _segsum_bf16(mid, ones_blk) + _segsum_bf16(lo, ones_blk)


def _pad_rows(x, rows):
    if x.shape[0] == rows:
        return x
    return jnp.concatenate([x, jnp.zeros((rows - x.shape[0],) + x.shape[1:], x.dtype)], axis=0)


def _ada_kernel(c_ref, w_ref, b_ref, o_ref):
    c = c_ref[...]
    o_ref[...] = _dot(_silu(c), w_ref[...]) + b_ref[...]


def _ada_mod(c_all, w_ada, b_ada, tn=512):
    n_layers, d, n = w_ada.shape
    rows = c_all.shape[0]
    return pl.pallas_call(
        _ada_kernel,
        grid=(n_layers, n // tn),
        in_specs=[pl.BlockSpec((rows, d), lambda l, j: (0, 0)),
                  pl.BlockSpec((None, d, tn), lambda l, j: (l, 0, j)),
                  pl.BlockSpec((None, 1, tn), lambda l, j: (l, 0, j))],
        out_specs=pl.BlockSpec((None, rows, tn), lambda l, j: (l, 0, j)),
        out_shape=jax.ShapeDtypeStruct((n_layers, rows, n), F32),
        compiler_params=_cparams(("arbitrary", "arbitrary")),
        name="ada_mod",
    )(c_all, w_ada, b_ada.reshape(n_layers, 1, n))


def _row_blocks(B, T, max_rows):
    if B == 1:
        return 1, min(T, max_rows)
    bb = max(1, min(B, max_rows // T))
    assert B % bb == 0
    return bb, T


def _modnorm(x, g, scale, shift):
    ms = jnp.mean(x * x, axis=-1, keepdims=True)
    h = x * lax.rsqrt(ms + EPS) * g
    return h * (1.0 + scale) + shift


def _modnorm_kernel(x_ref, g_ref, sc_ref, sh_ref, o_ref):
    o_ref[...] = _modnorm(x_ref[...], g_ref[...], sc_ref[...], sh_ref[...]).astype(o_ref.dtype)


def _modnorm_call(x, g, scale, shift):
    B, T, D = x.shape
    bb, tt = _row_blocks(B, T, 512)
    return pl.pallas_call(
        _modnorm_kernel,
        grid=(B // bb, T // tt),
        in_specs=[pl.BlockSpec((bb, tt, D), lambda b, t: (b, t, 0)),
                  pl.BlockSpec((1, D), lambda b, t: (0, 0)),
                  pl.BlockSpec((bb, 1, D), lambda b, t: (b, 0, 0)),
                  pl.BlockSpec((bb, 1, D), lambda b, t: (b, 0, 0))],
        out_specs=pl.BlockSpec((bb, tt, D), lambda b, t: (b, t, 0)),
        out_shape=jax.ShapeDtypeStruct((B, T, D), BF16),
        compiler_params=_cparams(("parallel", "parallel")),
        name="modnorm",
    )(x, g.reshape(1, D), scale, shift)


def _mm_kernel(x_ref, w_ref, o_ref):
    bb, tt, K = x_ref.shape
    acc = jnp.dot(x_ref[...].reshape(bb * tt, K), w_ref[...], preferred_element_type=F32)
    o_ref[...] = acc.reshape(o_ref.shape).astype(o_ref.dtype)


def _mm_res_kernel(x_ref, w_ref, res_ref, gate_ref, o_ref):
    bb, tt, K = x_ref.shape
    acc = jnp.dot(x_ref[...].reshape(bb * tt, K), w_ref[...], preferred_element_type=F32)
    o_ref[...] = res_ref[...] + gate_ref[...] * acc.reshape(o_ref.shape)


def _mm(x, w, tn, out_dtype=F32, res=None, gate=None, name="mm"):
    B, T, K = x.shape
    N = w.shape[1]
    bb, tt = _row_blocks(B, T, 1024)
    in_specs = [pl.BlockSpec((bb, tt, K), lambda b, t, j: (b, t, 0)),
                pl.BlockSpec((K, tn), lambda b, t, j: (0, j))]
    args = [x, w]
    kern = _mm_kernel
    if res is not None:
        in_specs += [pl.BlockSpec((bb, tt, tn), lambda b, t, j: (b, t, j)),
                     pl.BlockSpec((bb, 1, tn), lambda b, t, j: (b, 0, j))]
        args += [res, gate]
        kern = _mm_res_kernel
    return pl.pallas_call(
        kern,
        grid=(B // bb, T // tt, N // tn),
        in_specs=in_specs,
        out_specs=pl.BlockSpec((bb, tt, tn), lambda b, t, j: (b, t, j)),
        out_shape=jax.ShapeDtypeStruct((B, T, N), out_dtype),
        compiler_params=_cparams(("parallel", "parallel", "arbitrary")),
        name=name,
    )(*args)


def _merge_kernel(hd_ref, ya_ref, yb_ref, yc_ref, yd_ref, wmu_ref, bmg_ref, wb_ref, o_ref):
    bb, tt, _ = hd_ref.shape
    rows = bb * tt
    hd = hd_ref[...].reshape(rows, MERGE_RANK).astype(BF16)
    acc = None
    for i, y_ref in enumerate((ya_ref, yb_ref, yc_ref, yd_ref)):
        gate = _sigmoid(jnp.dot(hd, wmu_ref[i], preferred_element_type=F32) + bmg_ref[i])
        term = gate * jnp.dot(y_ref[...].reshape(rows, BRANCH_W), wb_ref[i], preferred_element_type=F32)
        acc = term if acc is None else acc + term
    o_ref[...] = acc.reshape(o_ref.shape).astype(o_ref.dtype)


def _merge(p_small, ys, wmu, bmg, wb, tn=512):
    B, T, _ = p_small.shape
    D = wb.shape[2]
    bb, tt = _row_blocks(B, T, 512)
    yspec = pl.BlockSpec((bb, tt, BRANCH_W), lambda b, t, j: (b, t, 0))
    return pl.pallas_call(
        _merge_kernel,
        grid=(B // bb, T // tt, D // tn),
        in_specs=[pl.BlockSpec((bb, tt, MERGE_RANK), lambda b, t, j: (b, t, 1)),
                  yspec, yspec, yspec, yspec,
                  pl.BlockSpec((4, MERGE_RANK, tn), lambda b, t, j: (0, 0, j)),
                  pl.BlockSpec((4, 1, tn), lambda b, t, j: (0, 0, j)),
                  pl.BlockSpec((4, BRANCH_W, tn), lambda b, t, j: (0, 0, j))],
        out_specs=pl.BlockSpec((bb, tt, tn), lambda b, t, j: (b, t, j)),
        out_shape=jax.ShapeDtypeStruct((B, T, D), BF16),
        compiler_params=_cparams(("parallel", "parallel", "arbitrary")),
        name="merge",
    )(p_small, *ys, wmu, bmg, wb)


def _route_kernel(x_ref, g_ref, sc_ref, sh_ref, wh_ref, wl_ref, br_ref, h_ref, r_ref):
    bb, tt, D = x_ref.shape
    rows = bb * tt
    h = _modnorm(x_ref[...], g_ref[...], sc_ref[...], sh_ref[...])
    h_ref[...] = h.astype(h_ref.dtype)
    h = h.reshape(rows, D)
    hh, hl = _split2(h)
    f = lambda u, v: lax.dot_general(u, v, NT_DIMS, preferred_element_type=F32)
    lg = f(wh_ref[...], hh) + f(wh_ref[...], hl) + f(wl_ref[...], hh) + br_ref[...]
    gl = [lg[i:i + 1, :] for i in range(N_GROUPS)]
    el = [lg[N_GROUPS + i:N_GROUPS + i + 1, :] for i in range(N_EXPERTS)]
    best = gl[0]
    gidx = jnp.zeros(best.shape, jnp.int32)
    for i in range(1, N_GROUPS):
        m = gl[i] > best
        best = jnp.where(m, gl[i], best)
        gidx = jnp.where(m, i, gidx)
    den = None
    for i in range(N_GROUPS):
        e = jnp.exp(gl[i] - best)
        den = e if den is None else den + e
    p_grp = 1.0 / den
    e_in = []
    for j in range(EXP_PER_GROUP):
        v = el[j]
        for g in range(1, N_GROUPS):
            v = jnp.where(gidx == g, el[g * EXP_PER_GROUP + j], v)
        e_in.append(v)
    t1 = e_in[0]
    i1 = jnp.zeros(best.shape, jnp.int32)
    for j in range(1, EXP_PER_GROUP):
        m = e_in[j] > t1
        t1 = jnp.where(m, e_in[j], t1)
        i1 = jnp.where(m, j, i1)
    cand = [jnp.where(i1 == j, -jnp.inf, e_in[j]) for j in range(EXP_PER_GROUP)]
    t2 = cand[0]
    i2 = jnp.zeros(best.shape, jnp.int32)
    for j in range(1, EXP_PER_GROUP):
        m = cand[j] > t2
        t2 = jnp.where(m, cand[j], t2)
        i2 = jnp.where(m, j, i2)
    e2 = jnp.exp(t2 - t1)
    w1 = p_grp / (1.0 + e2)
    w2 = p_grp * e2 / (1.0 + e2)
    r_ref[0:1, :] = (gidx * EXP_PER_GROUP + i1).astype(F32)
    r_ref[1:2, :] = (gidx * EXP_PER_GROUP + i2).astype(F32)
    r_ref[2:3, :] = w1
    r_ref[3:4, :] = w2
    r_ref[4:8, :] = jnp.zeros((4, rows), F32)


def _route(x, g, scale, shift, wr_hi, wr_lo, br):
    B, T, D = x.shape
    bb, tt = _row_blocks(B, T, 512)
    nT = T // tt
    rows = bb * tt
    return pl.pallas_call(
        _route_kernel,
        grid=(B // bb, nT),
        in_specs=[pl.BlockSpec((bb, tt, D), lambda b, t: (b, t, 0)),
                  pl.BlockSpec((1, D), lambda b, t: (0, 0)),
                  pl.BlockSpec((bb, 1, D), lambda b, t: (b, 0, 0)),
                  pl.BlockSpec((bb, 1, D), lambda b, t: (b, 0, 0)),
                  pl.BlockSpec((128, D), lambda b, t: (0, 0)),
                  pl.BlockSpec((128, D), lambda b, t: (0, 0)),
                  pl.BlockSpec((128, 1), lambda b, t: (0, 0))],
        out_specs=[pl.BlockSpec((bb, tt, D), lambda b, t: (b, t, 0)),
                   pl.BlockSpec((8, rows), lambda b, t: (0, b * nT + t))],
        out_shape=[jax.ShapeDtypeStruct((B, T, D), F32),
                   jax.ShapeDtypeStruct((8, B * T), F32)],
        compiler_params=_cparams(("parallel", "parallel")),
        name="route",
    )(x, g.reshape(1, D), scale, shift, wr_hi, wr_lo, br)


def _row_copy(src_hbm, dst_vmem, sem, src_row, dst_row):
    return pltpu.make_async_copy(src_hbm.at[pl.ds(src_row, 1), :], dst_vmem.at[pl.ds(dst_row, 1), :], sem)


def _experts_kernel(te_ref, tv_ref, tok_ref, h_hbm, wg_ref, wu_ref, wd_ref, o_ref, stage, xbf, sem):
    t = pl.program_id(0)
    f = pl.program_id(1)
    nt = pl.num_programs(0)
    tm = stage.shape[0]

    def gather(tile, start):
        def body(r, c):
            cp = _row_copy(h_hbm, stage, sem.at[0], tok_ref[tile * tm + r], r)
            if start:
                cp.start()
            else:
                cp.wait()
            return c
        lax.fori_loop(0, tm, body, 0, unroll=8)

    @pl.when(f == 0)
    def _():
        o_ref[...] = jnp.zeros(o_ref.shape, F32)

        @pl.when((t == 0) & (tv_ref[0] > 0))
        def _():
            gather(0, True)

        @pl.when(tv_ref[t] > 0)
        def _():
            gather(t, False)
            xbf[...] = stage[...].astype(BF16)

        nxt = jnp.minimum(t + 1, nt - 1)

        @pl.when((t + 1 < nt) & (tv_ref[nxt] > 0))
        def _():
            gather(nxt, True)

    @pl.when(tv_ref[t] > 0)
    def _():
        x = xbf[...]
        hg = jnp.dot(x, wg_ref[...].astype(BF16), preferred_element_type=F32)
        hu = jnp.dot(x, wu_ref[...].astype(BF16), preferred_element_type=F32)
        hid = (_silu(hg) * hu).astype(BF16)
        o_ref[...] += jnp.dot(hid, wd_ref[...].astype(BF16), preferred_element_type=F32)


def _experts(tile_expert, tile_valid, token_of_pos, h, w_gate, w_up, w_down, layer, tm, fc=256):
    Mpad = token_of_pos.shape[0]
    D = h.shape[1]
    nf = D_FF // fc
    n_tiles = Mpad // tm
    w_gate = w_gate.reshape((-1,) + w_gate.shape[2:])
    w_up = w_up.reshape((-1,) + w_up.shape[2:])
    w_down = w_down.reshape((-1,) + w_down.shape[2:])
    e0 = layer * N_EXPERTS

    def w_in_map(t, f, te, tv, tok):
        return (e0 + te[t], 0, jnp.where(tv[t] > 0, f, nf - 1))

    def w_out_map(t, f, te, tv, tok):
        return (e0 + te[t], jnp.where(tv[t] > 0, f, nf - 1), 0)

    grid_spec = pltpu.PrefetchScalarGridSpec(
        num_scalar_prefetch=3,
        grid=(n_tiles, nf),
        in_specs=[pl.BlockSpec(memory_space=pl.ANY),
                  pl.BlockSpec((None, D, fc), w_in_map),
                  pl.BlockSpec((None, D, fc), w_in_map),
                  pl.BlockSpec((None, fc, D), w_out_map)],
        out_specs=pl.BlockSpec((tm, D), lambda t, f, te, tv, tok: (t, 0)),
        scratch_shapes=[pltpu.VMEM((tm, D), F32), pltpu.VMEM((tm, D), BF16),
                        pltpu.SemaphoreType.DMA((1,))],
    )
    return pl.pallas_call(
        _experts_kernel,
        grid_spec=grid_spec,
        out_shape=jax.ShapeDtypeStruct((Mpad, D), F32),
        compiler_params=_cparams(("arbitrary", "arbitrary")),
        name="experts",
    )(tile_expert, tile_valid, token_of_pos, h, w_gate, w_up, w_down)


def _combine_kernel(pos_ref, x_ref, gate_ref, w0_ref, w1_ref, gf_ref, y_hbm, o_ref, gbuf, sem, *, final):
    bb, tt, D = x_ref.shape
    rows = bb * tt
    nT = pl.num_programs(1)
    i = pl.program_id(0) * nT + pl.program_id(1)
    n = pl.num_programs(0) * nT
    M = n * rows

    def gather(tile, slot, start):
        def body(r, c):
            for k in range(2):
                cp = _row_copy(y_hbm, gbuf.at[slot, k], sem.at[slot], pos_ref[k * M + tile * rows + r], r)
                if start:
                    cp.start()
                else:
                    cp.wait()
            return c
        lax.fori_loop(0, rows, body, 0, unroll=8)

    @pl.when(i == 0)
    def _():
        gather(0, 0, True)

    for slot in range(2):
        @pl.when(i % 2 == slot)
        def _():
            @pl.when(i + 1 < n)
            def _():
                gather(i + 1, 1 - slot, True)

            gather(i, slot, False)
            g0 = gbuf[slot, 0].reshape(bb, tt, D)
            g1 = gbuf[slot, 1].reshape(bb, tt, D)
            x = x_ref[...] + gate_ref[...] * (w0_ref[...] * g0 + w1_ref[...] * g1)
            if final:
                ms = jnp.mean(x * x, axis=-1, keepdims=True)
                x = x * lax.rsqrt(ms + EPS) * gf_ref[...]
            o_ref[...] = x


def _combine(x, gate, y, pos, w0, w1, g_final=None):
    B, T, D = x.shape
    bb, tt = _row_blocks(B, T, 256)
    assert B == 1 or tt == T
    rows = bb * tt
    big = pl.BlockSpec((bb, tt, D), lambda b, t, p: (b, t, 0))
    col = pl.BlockSpec((bb, tt, 1), lambda b, t, p: (b, t, 0))
    gf = jnp.ones((1, D), F32) if g_final is None else g_final.reshape(1, D)
    grid_spec = pltpu.PrefetchScalarGridSpec(
        num_scalar_prefetch=1,
        grid=(B // bb, T // tt),
        in_specs=[big, pl.BlockSpec((bb, 1, D), lambda b, t, p: (b, 0, 0)), col, col,
                  pl.BlockSpec((1, D), lambda b, t, p: (0, 0)),
                  pl.BlockSpec(memory_space=pl.ANY)],
        out_specs=big,
        scratch_shapes=[pltpu.VMEM((2, 2, rows, D), F32), pltpu.SemaphoreType.DMA((2,))],
    )
    return pl.pallas_call(
        functools.partial(_combine_kernel, final=g_final is not None),
        grid_spec=grid_spec,
        out_shape=jax.ShapeDtypeStruct((B, T, D), F32),
        compiler_params=_cparams(("arbitrary", "arbitrary")),
        name="combine",
    )(pos, x, gate, w0, w1, gf, y)


def _moe(x1, gate2, g_norm2, scale2, shift2, wr_hi, wr_lo, br, w_e_gate, w_e_up, w_e_down, layer, g_final):
    B, T, D = x1.shape
    M = B * T
    h2, route = _route(x1, g_norm2, scale2, shift2, wr_hi, wr_lo, br)
    tm = 512 if M >= 4096 else 128
    eid = route[0:2].astype(jnp.int32).reshape(2 * M)
    wts = route[2:4]
    onehot = (eid[:, None] == jnp.arange(N_EXPERTS, dtype=jnp.int32)[None, :]).astype(jnp.int32)
    csum = jnp.cumsum(onehot, axis=0)
    counts = csum[-1]
    rank = jnp.sum((csum - onehot) * onehot, axis=1)
    padded = ((counts + tm - 1) // tm) * tm
    starts = jnp.cumsum(padded) - padded
    pos = starts[eid] + rank
    n_tiles = (2 * M) // tm + N_EXPERTS
    Mpad = n_tiles * tm
    token = jnp.tile(jnp.arange(M, dtype=jnp.int32), 2)
    token_of_pos = jnp.zeros((Mpad,), jnp.int32).at[pos].set(token)
    tile_start = jnp.arange(n_tiles, dtype=jnp.int32) * tm
    ends = starts + padded
    tile_expert = jnp.sum((tile_start[:, None] >= ends[None, :]).astype(jnp.int32), axis=1)
    tile_valid = (tile_expert < N_EXPERTS).astype(jnp.int32)
    last_e = jnp.max(jnp.where(counts > 0, jnp.arange(N_EXPERTS, dtype=jnp.int32), 0))
    tile_expert = jnp.where(tile_valid > 0, tile_expert, last_e).astype(jnp.int32)
    out = _experts(tile_expert, tile_valid, token_of_pos, h2.reshape(M, D), w_e_gate, w_e_up, w_e_down,
                   layer, tm)
    w0 = wts[0].reshape(B, T, 1)
    w1 = wts[1].reshape(B, T, 1)
    return _combine(x1, gate2, out, pos.astype(jnp.int32), w0, w1, g_final)


def _conv4(ext_ref, x, cw_ref, cb_ref, rows):
    ext_ref[8:8 + rows, :] = x
    out = cb_ref[...] + ext_ref[5:5 + rows, :] * cw_ref[0:1, :]
    out = out + ext_ref[6:6 + rows, :] * cw_ref[1:2, :]
    out = out + ext_ref[7:7 + rows, :] * cw_ref[2:3, :]
    return out + x * cw_ref[3:4, :]


def _lru_kernel(lx_ref, lg_ref, buf_ref, h0_ref, cw_ref, cb_ref, wa_ref, ba_ref, wx_ref, bx_ref, lam_ref,
                y_ref, nbuf_ref, hl_ref, ext_scr, a_scr, u_scr, h_scr, hc_scr):
    t = pl.program_id(1)
    Tc = lx_ref.shape[0]

    @pl.when(t == 0)
    def _():
        ext_scr[0:5, :] = jnp.zeros((5, BRANCH_W), F32)
        ext_scr[5:8, :] = buf_ref[...]
        hc_scr[...] = h0_ref[...]

    xc = _conv4(ext_scr, lx_ref[...], cw_ref, cb_ref, Tc)
    tail = ext_scr[Tc + 5:Tc + 8, :]
    ext_scr[5:8, :] = tail
    nbuf_ref[...] = tail

    def blockdiag(w_ref):
        return jnp.concatenate(
            [_dot(xc[:, n * LRU_BW:(n + 1) * LRU_BW], w_ref[n]) for n in range(LRU_BLOCKS)], axis=1)

    r = _sigmoid(blockdiag(wa_ref) + ba_ref[...])
    i = _sigmoid(blockdiag(wx_ref) + bx_ref[...])
    log_a = LRU_C * r * _log_sigmoid(lam_ref[...])
    a_scr[...] = jnp.exp(log_a)
    u_scr[...] = jnp.sqrt(-jnp.tanh(log_a) * (jnp.exp(2.0 * log_a) + 1.0)) * (i * xc)

    def body(s, h):
        h = a_scr[pl.ds(s, 1), :] * h + u_scr[pl.ds(s, 1), :]
        h_scr[pl.ds(s, 1), :] = h
        return h

    h = lax.fori_loop(0, Tc, body, hc_scr[...], unroll=8)
    hc_scr[...] = h
    hl_ref[...] = h
    y_ref[...] = (_gelu_tanh(lg_ref[...]) * h_scr[...]).astype(y_ref.dtype)


def _lru(p_lru, buf, h0, conv_w, conv_b, wa, ba, wx, bx, lam):
    B, T, _ = p_lru.shape
    W = BRANCH_W
    Tc = min(T, 256)
    row = lambda a: a.reshape(1, W)
    const = lambda shape: pl.BlockSpec(shape, lambda b, t: (0,) * len(shape))
    y, nbuf, hl = pl.pallas_call(
        _lru_kernel,
        grid=(B, T // Tc),
        in_specs=[pl.BlockSpec((None, Tc, W), lambda b, t: (b, t, 0)),
                  pl.BlockSpec((None, Tc, W), lambda b, t: (b, t, 1)),
                  pl.BlockSpec((None, CONV_W - 1, W), lambda b, t: (b, 0, 0)),
                  pl.BlockSpec((None, 1, W), lambda b, t: (b, 0, 0)),
                  const((CONV_W, W)), const((1, W)),
                  const((LRU_BLOCKS, LRU_BW, LRU_BW)), const((1, W)),
                  const((LRU_BLOCKS, LRU_BW, LRU_BW)), const((1, W)), const((1, W))],
        out_specs=[pl.BlockSpec((None, Tc, W), lambda b, t: (b, t, 0)),
                   pl.BlockSpec((None, CONV_W - 1, W), lambda b, t: (b, 0, 0)),
                   pl.BlockSpec((None, 1, W), lambda b, t: (b, 0, 0))],
        out_shape=[jax.ShapeDtypeStruct((B, T, W), BF16),
                   jax.ShapeDtypeStruct((B, CONV_W - 1, W), F32),
                   jax.ShapeDtypeStruct((B, 1, W), F32)],
        scratch_shapes=[pltpu.VMEM((Tc + 8, W), F32), pltpu.VMEM((Tc, W), F32), pltpu.VMEM((Tc, W), F32),
                        pltpu.VMEM((Tc, W), F32), pltpu.VMEM((1, W), F32)],
        compiler_params=_cparams(("parallel", "arbitrary")),
        name="rglru",
    )(p_lru, p_lru, buf, h0.reshape(B, 1, W), conv_w, row(conv_b), wa, row(ba), wx, row(bx), row(lam))
    return y, nbuf, hl.reshape(B, W)


def _gla_kernel(q_ref, k_ref, v_ref, g_ref, ga_ref, s0_ref, wup_ref, wupt_ref, b_ref, bt_ref, gn_ref,
                y_ref, s_ref, s_scr):
    t = pl.program_id(1)
    nT = pl.num_programs(1)
    Tv = q_ref.shape[0]
    C = max(Tv, CHUNK)

    @pl.when(t == 0)
    def _():
        s_scr[...] = s0_ref[...]

    q = _pad_rows(q_ref[...], C) * (GLA_DK ** -0.5)
    k = _pad_rows(k_ref[...], C)
    v = _pad_rows(v_ref[...], C)
    ga = _pad_rows(ga_ref[...], C)
    la =_log_sigmoid(_dot_hl(ga, wup_ref[...]) + b_ref[...]) / GLA_NORMALIZER
    lat = _log_sigmoid(_dot_hl_nt(wupt_ref[...], ga) + bt_ref[...]) / GLA_NORMALIZER
    if Tv < C:
        la = jnp.where(lax.broadcasted_iota(jnp.int32, la.shape, 0) < Tv, la, 0.0)
        lat = jnp.where(lax.broadcasted_iota(jnp.int32, lat.shape, 1) < Tv, lat, 0.0)
    tri = jnp.where(_tri(C), 1.0, 0.0).astype(BF16)
    bc = _dot_3x(tri, la)
    b_last = bc[C - 1:C, :]
    b_mid = bc[C // 2:C // 2 + 1, :]
    b_last_col = jnp.sum(lat, axis=1, keepdims=True)
    q_in = q * jnp.exp(bc)
    q_att = q * jnp.exp(bc - b_mid)
    k_att = k * jnp.exp(b_mid - bc)
    k_dec = k * jnp.exp(b_last - bc)
    causal = _tri(C)
    gn = gn_ref[...]
    outs = []
    for h in range(GLA_HEADS):
        ks = slice(h * GLA_DK, (h + 1) * GLA_DK)
        vs = slice(h * GLA_DV, (h + 1) * GLA_DV)
        vh = v[:, vs]
        S = s_scr[h]
        att = jnp.where(causal, _dot_nt(q_att[:, ks], k_att[:, ks]), 0.0)
        o = _dot(att, vh) + _dot(q_in[:, ks], S)
        s_scr[h] = S * jnp.exp(b_last_col[ks, :]) + _dot_tn(k_dec[:, ks], vh)
        o = o * lax.rsqrt(jnp.mean(o * o, axis=-1, keepdims=True) + EPS) * gn
        outs.append(o)
    o = jnp.concatenate(outs, axis=1)[0:Tv]
    y_ref[...] = (o * _silu(g_ref[...])).astype(y_ref.dtype)

    @pl.when(t == nT - 1)
    def _():
        s_ref[...] = s_scr[...]


def _gla(p_gla, p_small, S0, w_up, b_a, g_norm):
    B, T, _ = p_gla.shape
    Tv = min(T, CHUNK)
    const = lambda shape: pl.BlockSpec(shape, lambda b, t: (0,) * len(shape))
    st = (GLA_HEADS, GLA_DK, GLA_DV)
    w_up_pad = jnp.concatenate([w_up, jnp.zeros((128 - GLA_RANK, GLA_QK), F32)], axis=0)
    y, S = pl.pallas_call(
        _gla_kernel,
        grid=(B, T // Tv),
        in_specs=[pl.BlockSpec((None, Tv, GLA_QK), lambda b, t: (b, t, 0)),
                  pl.BlockSpec((None, Tv, GLA_QK), lambda b, t: (b, t, 1)),
                  pl.BlockSpec((None, Tv, BRANCH_W), lambda b, t: (b, t, 1)),
                  pl.BlockSpec((None, Tv, BRANCH_W), lambda b, t: (b, t, 2)),
                  pl.BlockSpec((None, Tv, 128), lambda b, t: (b, t, 0)),
                  pl.BlockSpec((None,) + st, lambda b, t: (b, 0, 0, 0)),
                  const((128, GLA_QK)), const((GLA_QK, 128)),
                  const((1, GLA_QK)), const((GLA_QK, 1)), const((1, GLA_DV))],
        out_specs=[pl.BlockSpec((None, Tv, BRANCH_W), lambda b, t: (b, t, 0)),
                   pl.BlockSpec((None,) + st, lambda b, t: (b, 0, 0, 0))],
        out_shape=[jax.ShapeDtypeStruct((B, T, BRANCH_W), BF16),
                   jax.ShapeDtypeStruct((B,) + st, F32)],
        scratch_shapes=[pltpu.VMEM(st, F32)],
        compiler_params=_cparams(("parallel", "arbitrary")),
        name="gla",
    )(p_gla, p_gla, p_gla, p_gla, p_small, S0, w_up_pad, w_up_pad.T, b_a.reshape(1, GLA_QK),
      b_a.reshape(GLA_QK, 1), g_norm.reshape(1, GLA_DV))
    return y, S


def _mlstm_kernel(qk_ref, v_ref, o_ref, if_ref, ift_ref, buf_ref, c0_ref, n0_ref, m0_ref,
                  cw_ref, cb_ref, bif_ref, bift_ref, ng_ref,
                  y_ref, nbuf_ref, c_ref, n_ref, m_ref, ext_scr, c_scr, n_scr, m_scr):
    t = pl.program_id(1)
    nT = pl.num_programs(1)
    Tv = qk_ref.shape[0]
    C = max(Tv, CHUNK)
    W2 = 2 * BRANCH_W

    @pl.when(t == 0)
    def _():
        ext_scr[0:5, :] = jnp.zeros((5, W2), F32)
        ext_scr[5:8, :] = buf_ref[...]
        c_scr[...] = c0_ref[...]
        n_scr[...] = n0_ref[...]
        m_scr[...] = jnp.zeros(m_scr.shape, F32)
        m_scr[0:1, 0:ML_HEADS] = m0_ref[...]

    qk = _conv4(ext_scr, qk_ref[...], cw_ref, cb_ref, Tv)
    tail = ext_scr[Tv + 5:Tv + 8, :]
    ext_scr[5:8, :] = tail
    nbuf_ref[...] = tail
    qk = _pad_rows(_silu(qk), C)
    q = qk[:, 0:BRANCH_W]
    k = qk[:, BRANCH_W:W2] * (ML_DH ** -0.5)
    v = _pad_rows(v_ref[...], C)

    gates = _pad_rows(if_ref[...], C) + bif_ref[...]
    i_col = gates
    lf_col = _log_sigmoid(gates)
    gates_t = _pad_rows(ift_ref[...] + bift_ref[...], 16)
    if Tv < C:
        gates_t = jnp.concatenate([gates_t, jnp.zeros((16, C - Tv), F32)], axis=1)
    i_row = gates_t
    lf_row = _log_sigmoid(gates_t)
    if Tv < C:
        rmask = lax.broadcasted_iota(jnp.int32, (C, 128), 0) < Tv
        cmask = lax.broadcasted_iota(jnp.int32, (16, C), 1) < Tv
        i_col = jnp.where(rmask, i_col, NEG_BIG)
        lf_col = jnp.where(rmask, lf_col, 0.0)
        i_row = jnp.where(cmask, i_row, NEG_BIG)
        lf_row = jnp.where(cmask, lf_row, 0.0)
    causal = _tri(C)
    tri = jnp.where(causal, 1.0, 0.0).astype(BF16)
    triu = jnp.where(_tri(C, upper=True), 1.0, 0.0).astype(BF16)
    F_col = _dot_3x(tri, lf_col)
    F_row = _dot_x3(lf_row, triu)
    ng = ng_ref[...]
    outs = []
    for h in range(ML_HEADS):
        hs = slice(h * ML_DH, (h + 1) * ML_DH)
        qh, kh, vh = q[:, hs], k[:, hs], v[:, hs]
        Fc = F_col[:, 16 + ML_HEADS + h:17 + ML_HEADS + h]
        ic = i_col[:, 16 + h:17 + h]
        m_prev = m_scr[0:1, h:h + 1]
        Fr = F_row[ML_HEADS + h:ML_HEADS + h + 1, :]
        log_d = jnp.where(causal, Fc - Fr + i_row[h:h + 1, :], NEG_BIG)
        log_inter = Fc + m_prev
        m_t = jnp.maximum(log_inter, jnp.max(log_d, axis=1, keepdims=True))
        s = _dot_nt(qh, kh) * jnp.exp(log_d - m_t)
        inter = jnp.exp(log_inter - m_t)
        Ch = c_scr[h]
        nh = n_scr[h:h + 1, :]
        num = _dot(s, vh) + inter * _dot(qh, Ch)
        den = jnp.sum(s, axis=1, keepdims=True) + inter * jnp.sum(qh * nh, axis=1, keepdims=True)
        hh = num / jnp.maximum(jnp.abs(den), jnp.exp(-m_t))
        m_new = m_t[C - 1:C, :]
        F_last = Fc[C - 1:C, :]
        decay = jnp.exp(F_last + m_prev - m_new)
        w_s = jnp.exp(F_last - Fc + ic - m_new)
        kw = kh * w_s
        c_scr[h] = decay * Ch + _dot_tn(kw, vh)
        n_scr[h:h + 1, :] = decay * nh + jnp.sum(kw, axis=0, keepdims=True)
        m_scr[0:1, h:h + 1] = m_new
        hh = hh * lax.rsqrt(jnp.mean(hh * hh, axis=-1, keepdims=True) + EPS) * ng[:, hs]
        outs.append(hh)
    hcat = jnp.concatenate(outs, axis=1)[0:Tv]
    y_ref[...] = (hcat * _sigmoid(o_ref[...])).astype(y_ref.dtype)

    @pl.when(t == nT - 1)
    def _():
        c_ref[...] = c_scr[...]
        n_ref[...] = n_scr[...]
        m_ref[...] = m_scr[0:1, 0:ML_HEADS]


def _mlstm(p_ml, p_small, buf, C0, n0, m0, conv_w, conv_b, b_if, norm_g):
    B, T, _ = p_ml.shape
    W = BRANCH_W
    Tv = min(T, CHUNK)
    ift = jnp.swapaxes(p_small[:, :, 16:16 + 2 * ML_HEADS], 1, 2)
    const = lambda shape: pl.BlockSpec(shape, lambda b, t: (0,) * len(shape))
    cst = (ML_HEADS, ML_DH, ML_DH)
    y, nbuf, Cn, nn, mn = pl.pallas_call(
        _mlstm_kernel,
        grid=(B, T // Tv),
        in_specs=[pl.BlockSpec((None, Tv, 2 * W), lambda b, t: (b, t, 0)),
                  pl.BlockSpec((None, Tv, W), lambda b, t: (b, t, 2)),
                  pl.BlockSpec((None, Tv, W), lambda b, t: (b, t, 3)),
                  pl.BlockSpec((None, Tv, 128), lambda b, t: (b, t, 0)),
                  pl.BlockSpec((None, 2 * ML_HEADS, Tv), lambda b, t: (b, 0, t)),
                  pl.BlockSpec((None, CONV_W - 1, 2 * W), lambda b, t: (b, 0, 0)),
                  pl.BlockSpec((None,) + cst, lambda b, t: (b, 0, 0, 0)),
                  pl.BlockSpec((None, ML_HEADS, ML_DH), lambda b, t: (b, 0, 0)),
                  pl.BlockSpec((None, 1, ML_HEADS), lambda b, t: (b, 0, 0)),
                  const((CONV_W, 2 * W)), const((1, 2 * W)),
                  const((1, 128)), const((2 * ML_HEADS, 1)), const((1, W))],
        out_specs=[pl.BlockSpec((None, Tv, W), lambda b, t: (b, t, 0)),
                   pl.BlockSpec((None, CONV_W - 1, 2 * W), lambda b, t: (b, 0, 0)),
                   pl.BlockSpec((None,) + cst, lambda b, t: (b, 0, 0, 0)),
                   pl.BlockSpec((None, ML_HEADS, ML_DH), lambda b, t: (b, 0, 0)),
                   pl.BlockSpec((None, 1, ML_HEADS), lambda b, t: (b, 0, 0))],
        out_shape=[jax.ShapeDtypeStruct((B, T, W), BF16),
                   jax.ShapeDtypeStruct((B, CONV_W - 1, 2 * W), F32),
                   jax.ShapeDtypeStruct((B,) + cst, F32),
                   jax.ShapeDtypeStruct((B, ML_HEADS, ML_DH), F32),
                   jax.ShapeDtypeStruct((B, 1, ML_HEADS), F32)],
        scratch_shapes=[pltpu.VMEM((Tv + 8, 2 * W), F32), pltpu.VMEM(cst, F32),
                        pltpu.VMEM((ML_HEADS, ML_DH), F32), pltpu.VMEM((8, 128), F32)],
        compiler_params=_cparams(("parallel", "arbitrary")),
        name="mlstm",
    )(p_ml, p_ml, p_ml, p_small, ift, buf, C0, n0, m0.reshape(B, 1, ML_HEADS),
      conv_w, conv_b.reshape(1, 2 * W),
      jnp.zeros((1, 128), F32).at[0, 16:16 + 2 * ML_HEADS].set(b_if), b_if.reshape(2 * ML_HEADS, 1),
      norm_g.reshape(1, W))
    return y, nbuf, Cn, nn, mn.reshape(B, ML_HEADS)


def _rwkv_kernel(p_ref, sh_ref, s0_ref, mu_ref, w0_ref, wup_ref, a0_ref, aup_ref, gup_ref,
                 kk_ref, ka_ref, rk_ref, lng_ref, lnb_ref,
                 y_ref, shn_ref, s_ref,
                 ext_scr, nkk_scr, w_scr, kka_scr, k2_scr, x_scr, v_scr, g_scr, bon_scr, yv_scr,
                 st_scr, yr_scr):
    t = pl.program_id(1)
    nT = pl.num_programs(1)
    Tb = p_ref.shape[0]
    W = BRANCH_W

    @pl.when(t == 0)
    def _():
        ext_scr[0:7, :] = jnp.zeros((7, RW_NCOLS), F32)
        ext_scr[7:8, :] = sh_ref[...]
        st_scr[...] = s0_ref[...]

    ones_blk = _seg_ones(256, RW_DH)
    p = p_ref[...]
    ext_scr[8:8 + Tb, :] = p
    prev = ext_scr[7:7 + Tb, :]
    last = p[Tb - 1:Tb, :]
    ext_scr[7:8, :] = last
    shn_ref[...] = last
    pm = p + (prev - p) * mu_ref[...]
    r = pm[:, 0:W]
    k = pm[:, W:2 * W]
    v = pm[:, 2 * W:3 * W]
    wa_d = pm[:, 3 * W:3 * W + 128]
    gd = pm[:, 3 * W + 128:3 * W + 256]
    log_w = -RW_DECAY_SCALE * _sigmoid(w0_ref[...] + _dot(jnp.tanh(wa_d), wup_ref[...]))
    a = _sigmoid(a0_ref[...] + _dot(wa_d, aup_ref[...]))
    g_scr[...] = _dot(_sigmoid(gd), gup_ref[...])
    kkr = k * kk_ref[...]
    kk = kkr / jnp.maximum(jnp.sqrt(_segsum3(kkr * kkr, ones_blk)), 1e-12)
    k2 = k * (1.0 + (a - 1.0) * ka_ref[...])
    bon_scr[...] = _segsum3(r * k2 * rk_ref[...], ones_blk) * v
    w = jnp.exp(log_w)
    kka = kk * a
    c1 = _segsum3(kka * r, ones_blk)
    yv_scr[...] = _segsum3(k2 * r, ones_blk) * v
    x_scr[...] = w * r - kk * c1
    nkk_scr[...] = -kk
    w_scr[...] = w
    kka_scr[...] = kka
    k2_scr[...] = k2
    v_scr[...] = v

    BL = 256
    eye_t = (lax.broadcasted_iota(jnp.int32, (RW_DH, BL), 0)
             == lax.broadcasted_iota(jnp.int32, (RW_DH, BL), 1) % RW_DH)
    eye_f = jnp.where(eye_t, 1.0, 0.0)
    NG = W // BL

    def step(row, c):
        r1 = pl.ds(row, 1)
        S = [st_scr[:, pl.ds(gi * BL, BL)] for gi in range(NG)]
        for half in range(2):
            gis = list(range(half * NG // 2, (half + 1) * NG // 2))
            lss = [pl.ds(gi * BL, BL) for gi in gis]
            lhs = ([S[gi] * nkk_scr[r1, ls] for gi, ls in zip(gis, lss)]
                   + [eye_f * v_scr[r1, ls] for ls in lss]
                   + [S[gi] * x_scr[r1, ls] for gi, ls in zip(gis, lss)])
            res = jnp.dot(jnp.concatenate(lhs, axis=0).astype(BF16), ones_blk, preferred_element_type=F32)
            n = len(gis)
            for j, gi in enumerate(gis):
                ls = lss[j]
                sa = res[j * RW_DH:(j + 1) * RW_DH]
                vc = res[(n + j) * RW_DH:(n + j + 1) * RW_DH]
                yq = res[(2 * n + j) * RW_DH:(2 * n + j + 1) * RW_DH]
                st_scr[:, ls] = S[gi] * w_scr[r1, ls] + sa * kka_scr[r1, ls] + vc * k2_scr[r1, ls]
                yr_scr[r1, ls] = jnp.sum(yq * eye_f, axis=0, keepdims=True)
        return c

    lax.fori_loop(0, Tb, step, 0, unroll=4)

    y = yr_scr[...] + yv_scr[...]
    mean = _segsum3(y, ones_blk) * (1.0 / RW_DH)
    dlt = y - mean
    var = _segsum3(dlt * dlt, ones_blk) * (1.0 / RW_DH)
    yn = dlt * lax.rsqrt(var + RW_LN_EPS) * lng_ref[...] + lnb_ref[...]
    y_ref[...] = ((yn + bon_scr[...]) * g_scr[...]).astype(y_ref.dtype)

    @pl.when(t == nT - 1)
    def _():
        s_ref[...] = st_scr[...]


def _rwkv(p_rw, shift_prev, S0, mu, w0, w_up, a0, a_up, g_up, k_k, k_a, r_k, ln_g, ln_b):
    B, T, _ = p_rw.shape
    W = BRANCH_W
    Tb = min(T, 256)
    s0 = jnp.transpose(S0, (0, 2, 1, 3)).reshape(B, RW_DH, W)
    row = lambda a: a.reshape(1, -1)
    const = lambda shape: pl.BlockSpec(shape, lambda b, t: (0,) * len(shape))
    rows = lambda: pltpu.VMEM((Tb, W), F32)
    y, shn, S = pl.pallas_call(
        _rwkv_kernel,
        grid=(B, T // Tb),
        in_specs=[pl.BlockSpec((None, Tb, RW_NCOLS), lambda b, t: (b, t, 0)),
                  pl.BlockSpec((None, 1, RW_NCOLS), lambda b, t: (b, 0, 0)),
                  pl.BlockSpec((None, RW_DH, W), lambda b, t: (b, 0, 0)),
                  const((1, RW_NCOLS)), const((1, W)), const((128, W)), const((1, W)), const((128, W)),
                  const((128, W)), const((1, W)), const((1, W)), const((1, W)), const((1, W)), const((1, W))],
        out_specs=[pl.BlockSpec((None, Tb, W), lambda b, t: (b, t, 0)),
                   pl.BlockSpec((None, 1, RW_NCOLS), lambda b, t: (b, 0, 0)),
                   pl.BlockSpec((None, RW_DH, W), lambda b, t: (b, 0, 0))],
        out_shape=[jax.ShapeDtypeStruct((B, T, W), BF16),
                   jax.ShapeDtypeStruct((B, 1, RW_NCOLS), F32),
                   jax.ShapeDtypeStruct((B, RW_DH, W), F32)],
        scratch_shapes=[pltpu.VMEM((Tb + 8, RW_NCOLS), F32),
                        rows(), rows(), rows(), rows(), rows(), rows(), rows(), rows(), rows(),
                        pltpu.VMEM((RW_DH, W), F32), rows()],
        compiler_params=_cparams(("parallel", "arbitrary")),
        name="rwkv7",
    )(p_rw, shift_prev.reshape(B, 1, RW_NCOLS), s0, row(mu), row(w0),
      jnp.concatenate([w_up, jnp.zeros_like(w_up)], axis=0), row(a0),
      jnp.concatenate([jnp.zeros_like(a_up), a_up], axis=0), g_up,
      row(k_k), row(k_a), row(r_k), row(ln_g), row(ln_b))
    S = jnp.transpose(S.reshape(B, RW_DH, RW_HEADS, RW_DH), (0, 2, 1, 3))
    return y, shn.reshape(B, RW_NCOLS), S


def _prep_layer(l, w_in, w_mg_down, w_mg_up, b_mg, w_branch, w_out, w_route_g, b_route_g, w_route_e, b_route_e):
    W = BRANCH_W
    wi = w_in[l]
    offs = [0]
    for s in (GLA_QK, GLA_QK, W, W, GLA_RANK, RW_NCOLS, 2 * W, W, W, 2 * ML_HEADS, W, W):
        offs.append(offs[-1] + s)
    seg = lambda i: wi[:, offs[i]:offs[i + 1]]
    w_gla = jnp.concatenate([seg(0), seg(1), seg(2), seg(3)], axis=1).astype(BF16)
    w_rw = seg(5).astype(BF16)
    w_ml = jnp.concatenate([seg(6), seg(7), seg(8)], axis=1).astype(BF16)
    w_lru = jnp.concatenate([seg(10), seg(11)], axis=1).astype(BF16)
    pad = jnp.zeros((D_MODEL, 256 - GLA_RANK - 2 * ML_HEADS), F32)
    w_small = jnp.concatenate([seg(4), seg(9), pad, w_mg_down[l]], axis=1).astype(BF16)
    wmu = jnp.transpose(w_mg_up[l].reshape(MERGE_RANK, 4, D_MODEL), (1, 0, 2)).astype(BF16)
    bmg = b_mg[l].reshape(4, 1, D_MODEL)
    wr = jnp.concatenate([w_route_g[l], w_route_e[l]], axis=1).T
    wr = jnp.concatenate([wr, jnp.zeros((128 - wr.shape[0], D_MODEL), F32)], axis=0)
    wr_hi = wr.astype(BF16)
    wr_lo = (wr - wr_hi.astype(F32)).astype(BF16)
    br = jnp.concatenate([b_route_g[l], b_route_e[l], jnp.zeros((128 - 20,), F32)]).reshape(128, 1)
    return dict(w_gla=w_gla, w_rw=w_rw, w_ml=w_ml, w_lru=w_lru, w_small=w_small, wmu=wmu, bmg=bmg,
                wb=w_branch[l].astype(BF16), w_out=w_out[l].astype(BF16), wr_hi=wr_hi, wr_lo=wr_lo, br=br)


def _run_trunk(x, mod, states, lw, prep, g_final):
    n_layers = mod.shape[0]
    new_states = []
    for l in range(n_layers):
        pw = prep[l]
        st = [s[l] for s in states]
        gla_S, rw_S, rw_shift, ml_C, ml_n, ml_m, ml_conv, lru_h, lru_conv = st
        m = mod[l]
        shift1, scale1, gate1, shift2, scale2, gate2 = [m[:, i:i + 1, :] for i in range(6)]
        h = _modnorm_call(x, lw["g_norm1"][l], scale1, shift1)
        p_gla = _mm(h, pw["w_gla"], 512, name="proj_gla")
        p_rw = _mm(h, pw["w_rw"], 256, name="proj_rw")
        p_ml = _mm(h, pw["w_ml"], 512, name="proj_ml")
        p_lru = _mm(h, pw["w_lru"], 512, name="proj_lru")
        p_small = _mm(h, pw["w_small"], 512, name="proj_small")
        y_a, gla_S = _gla(p_gla, p_small, gla_S, lw["gla_w_up"][l], lw["gla_b"][l], lw["gla_g_norm"][l])
        y_b, rw_shift, rw_S = _rwkv(p_rw, rw_shift, rw_S, lw["rw_mu"][l], lw["rw_w0"][l], lw["rw_w_up"][l],
                                    lw["rw_a0"][l], lw["rw_a_up"][l], lw["rw_g_up"][l], lw["rw_k_k"][l],
                                    lw["rw_k_a"][l], lw["rw_r_k"][l], lw["rw_ln_g"][l], lw["rw_ln_b"][l])
        y_c, ml_conv, ml_C, ml_n, ml_m = _mlstm(p_ml, p_small, ml_conv, ml_C, ml_n, ml_m, lw["ml_conv_w"][l],
                                                lw["ml_conv_b"][l], lw["ml_b_if"][l], lw["ml_norm_g"][l])
        y_d, lru_conv, lru_h = _lru(p_lru, lru_conv, lru_h, lw["lru_conv_w"][l], lw["lru_conv_b"][l],
                                    lw["lru_wa"][l], lw["lru_ba"][l], lw["lru_wx"][l], lw["lru_bx"][l],
                                    lw["lru_lambda"][l])
        merged = _merge(p_small, (y_a, y_b, y_c, y_d), pw["wmu"], pw["bmg"], pw["wb"])
        x1 = _mm(merged, pw["w_out"], 512, res=x, gate=gate1, name="out_proj")
        x = _moe(x1, gate2, lw["g_norm2"][l], scale2, shift2, pw["wr_hi"], pw["wr_lo"], pw["br"],
                 lw["w_e_gate"], lw["w_e_up"], lw["w_e_down"], l,
                 g_final if l == n_layers - 1 else None)
        new_states.append((gla_S, rw_S, rw_shift, ml_C, ml_n, ml_m, ml_conv, lru_h, lru_conv))
    return x, [jnp.stack([st[i] for st in new_states]) for i in range(9)]


def kernel(x_prompt, x_sample, c_prompt, c_sample, state_gla_S, state_rwkv_S, state_rwkv_shift, state_mlstm_C, state_mlstm_n, state_mlstm_m, state_mlstm_conv, state_lru_h, state_lru_conv, w_ada, b_ada, g_norm1, g_norm2, w_in, gla_w_up, gla_b, gla_g_norm, rw_mu, rw_w0, rw_w_up, rw_a0, rw_a_up, rw_g_up, rw_k_k, rw_k_a, rw_r_k, rw_ln_g, rw_ln_b, ml_conv_w, ml_conv_b, ml_b_if, ml_norm_g, lru_conv_w, lru_conv_b, lru_wa, lru_ba, lru_wx, lru_bx, lru_lambda, w_branch, w_mg_down, w_mg_up, b_mg, w_out, w_route_g, b_route_g, w_route_e, b_route_e, w_e_gate, w_e_up, w_e_down, g_final):
    n_layers = w_ada.shape[0]
    Bp, Bs = x_prompt.shape[0], x_sample.shape[0]
    lw = dict(g_norm1=g_norm1, g_norm2=g_norm2, gla_w_up=gla_w_up, gla_b=gla_b, gla_g_norm=gla_g_norm,
              rw_mu=rw_mu, rw_w0=rw_w0, rw_w_up=rw_w_up, rw_a0=rw_a0, rw_a_up=rw_a_up, rw_g_up=rw_g_up,
              rw_k_k=rw_k_k, rw_k_a=rw_k_a, rw_r_k=rw_r_k, rw_ln_g=rw_ln_g, rw_ln_b=rw_ln_b,
              ml_conv_w=ml_conv_w, ml_conv_b=ml_conv_b, ml_b_if=ml_b_if, ml_norm_g=ml_norm_g,
              lru_conv_w=lru_conv_w, lru_conv_b=lru_conv_b, lru_wa=lru_wa, lru_ba=lru_ba, lru_wx=lru_wx,
              lru_bx=lru_bx, lru_lambda=lru_lambda, w_e_gate=w_e_gate, w_e_up=w_e_up, w_e_down=w_e_down)
    prep = [_prep_layer(l, w_in, w_mg_down, w_mg_up, b_mg, w_branch, w_out,
                        w_route_g, b_route_g, w_route_e, b_route_e) for l in range(n_layers)]
    nb = Bp + Bs
    rows = ((nb + 15) // 16) * 16
    c_all = jnp.concatenate([c_prompt, c_sample, jnp.zeros((rows - nb, D_MODEL), F32)], axis=0)
    mod = _ada_mod(c_all, w_ada, b_ada).reshape(n_layers, rows, 6, D_MODEL)
    mod_p = mod[:, 0:Bp]
    mod_s = mod[:, Bp:nb]

    def zeros(*s):
        return jnp.zeros((n_layers, Bp) + s, F32)

    zero_states = (zeros(GLA_HEADS, GLA_DK, GLA_DV), zeros(RW_HEADS, RW_DH, RW_DH), zeros(RW_NCOLS),
                   zeros(ML_HEADS, ML_DH, ML_DH), zeros(ML_HEADS, ML_DH), zeros(ML_HEADS),
                   zeros(CONV_W - 1, 2 * BRANCH_W), zeros(BRANCH_W), zeros(CONV_W - 1, BRANCH_W))
    sample_states = (state_gla_S, state_rwkv_S, state_rwkv_shift, state_mlstm_C, state_mlstm_n,
                     state_mlstm_m, state_mlstm_conv, state_lru_h, state_lru_conv)
    y_prompt, ps = _run_trunk(x_prompt, mod_p, zero_states, lw, prep, g_final)
    y_sample, ss = _run_trunk(x_sample, mod_s, sample_states, lw, prep, g_final)
    return (y_prompt, y_sample, *ps, *ss)
```

```python
import functools

import jax
import jax.numpy as jnp
from jax import lax
from jax.experimental import pallas as pl
from jax.experimental.pallas import tpu as pltpu

F32 = jnp.float32
BF16 = jnp.bfloat16

D_MODEL = 4096
BRANCH_W = D_MODEL // 4
EPS = 1e-6
GLA_HEADS = 4
GLA_DK = 128
GLA_DV = 256
GLA_QK = GLA_HEADS * GLA_DK
GLA_RANK = 16
GLA_NORMALIZER = 16.0
RW_DH = 64
RW_HEADS = BRANCH_W // RW_DH
RW_NCOLS = 3 * BRANCH_W + 256
RW_DECAY_SCALE = 0.606531
RW_LN_EPS = 64e-5
ML_HEADS = 4
ML_DH = BRANCH_W // ML_HEADS
CONV_W = 4
LRU_BLOCKS = 8
LRU_BW = BRANCH_W // LRU_BLOCKS
LRU_C = 8.0
N_GROUPS = 4
EXP_PER_GROUP = 4
N_EXPERTS = 16
D_FF = D_MODEL // 4
MERGE_RANK = 256

VMEM_LIMIT = 56 * 1024 * 1024
CHUNK = 128
NEG_BIG = -1e30

NT_DIMS = (((1,), (1,)), ((), ()))
TN_DIMS = (((0,), (0,)), ((), ()))


def _cparams(sem):
    return pltpu.CompilerParams(dimension_semantics=sem, vmem_limit_bytes=VMEM_LIMIT)


def _dot(a, b):
    return jnp.dot(a.astype(BF16), b.astype(BF16), preferred_element_type=F32)


def _dot_nt(a, b):
    return lax.dot_general(a.astype(BF16), b.astype(BF16), NT_DIMS, preferred_element_type=F32)


def _dot_tn(a, b):
    return lax.dot_general(a.astype(BF16), b.astype(BF16), TN_DIMS, preferred_element_type=F32)


def _split3(x):
    hi = x.astype(BF16)
    r1 = x - hi.astype(F32)
    mid = r1.astype(BF16)
    lo = (r1 - mid.astype(F32)).astype(BF16)
    return hi, mid, lo


def _split2(x):
    hi = x.astype(BF16)
    lo = (x - hi.astype(F32)).astype(BF16)
    return hi, lo


def _dot_x3(a, b_exact):
    hi, mid, lo = _split3(a)
    f = lambda u: jnp.dot(u, b_exact, preferred_element_type=F32)
    return f(hi) + f(mid) + f(lo)


def _dot_3x(a_exact, b):
    hi, mid, lo = _split3(b)
    f = lambda u: jnp.dot(a_exact, u, preferred_element_type=F32)
    return f(hi) + f(mid) + f(lo)


def _dot_hl(a, b):
    ah, al = _split2(a)
    bh, bl = _split2(b)
    f = lambda u, v: jnp.dot(u, v, preferred_element_type=F32)
    return f(ah, bh) + f(al, bh) + f(ah, bl)


def _dot_hl_nt(a, b):
    ah, al = _split2(a)
    bh, bl = _split2(b)
    f = lambda u, v: lax.dot_general(u, v, NT_DIMS, preferred_element_type=F32)
    return f(ah, bh) + f(al, bh) + f(ah, bl)


def _sigmoid(x):
    return 1.0 / (1.0 + jnp.exp(-x))


def _silu(x):
    return x * _sigmoid(x)


def _log_sigmoid(x):
    return jnp.minimum(x, 0.0) - jnp.log(1.0 + jnp.exp(-jnp.abs(x)))


def _gelu_tanh(x):
    return 0.5 * x * (1.0 + jnp.tanh(0.7978845608028654 * (x + 0.044715 * x * x * x)))


def _tri(n, upper=False):
    r = lax.broadcasted_iota(jnp.int32, (n, n), 0)
    c = lax.broadcasted_iota(jnp.int32, (n, n), 1)
    m = (r <= c) if upper else (r >= c)
    return m


def _seg_ones(width, seg):
    r = lax.broadcasted_iota(jnp.int32, (width, width), 0) // seg
    c = lax.broadcasted_iota(jnp.int32, (width, width), 1) // seg
    return jnp.where(r == c, 1.0, 0.0).astype(BF16)


def _segsum_bf16(x_bf16, ones_blk):
    bw = ones_blk.shape[0]
    n = x_bf16.shape[1] // bw
    return jnp.concatenate(
        [jnp.dot(x_bf16[:, g * bw:(g + 1) * bw], ones_blk, preferred_element_type=F32) for g in range(n)],
        axis=1)


def _segsum3(x, ones_blk):
    hi, mid, lo = _split3(x)
    return _segsum_bf16(hi, ones_blk) + _segsum_bf16(mid, ones_blk) + _segsum_bf16(lo, ones_blk)


def _pad_rows(x, rows):
    if x.shape[0] == rows:
        return x
    return jnp.concatenate([x, jnp.zeros((rows - x.shape[0],) + x.shape[1:], x.dtype)], axis=0)


def _ada_kernel(c_ref, w_ref, b_ref, o_ref):
    c = c_ref[...]
    o_ref[...] = _dot(_silu(c), w_ref[...]) + b_ref[...]


def _ada_mod(c_all, w_ada, b_ada, tn=1024):
    n_layers, d, n = w_ada.shape
    rows = c_all.shape[0]
    return pl.pallas_call(
        _ada_kernel,
        grid=(n_layers, n // tn),
        in_specs=[pl.BlockSpec((rows, d), lambda l, j: (0, 0)),
                  pl.BlockSpec((None, d, tn), lambda l, j: (l, 0, j)),
                  pl.BlockSpec((None, 1, tn), lambda l, j: (l, 0, j))],
        out_specs=pl.BlockSpec((None, rows, tn), lambda l, j: (l, 0, j)),
        out_shape=jax.ShapeDtypeStruct((n_layers, rows, n), F32),
        compiler_params=_cparams(("arbitrary", "arbitrary")),
        name="ada_mod",
    )(c_all, w_ada, b_ada.reshape(n_layers, 1, n))


def _row_blocks(B, T, max_rows):
    if B == 1:
        return 1, min(T, max_rows)
    bb = max(1, min(B, max_rows // T))
    assert B % bb == 0
    return bb, T


def _modnorm(x, g, scale, shift):
    ms = jnp.mean(x * x, axis=-1, keepdims=True)
    h = x * lax.rsqrt(ms + EPS) * g
    return h * (1.0 + scale) + shift


def _modnorm_kernel(x_ref, g_ref, sc_ref, sh_ref, o_ref):
    o_ref[...] = _modnorm(x_ref[...], g_ref[...], sc_ref[...], sh_ref[...]).astype(o_ref.dtype)


def _modnorm_call(x, g, scale, shift):
    B, T, D = x.shape
    bb, tt = _row_blocks(B, T, 512)
    return pl.pallas_call(
        _modnorm_kernel,
        grid=(B // bb, T // tt),
        in_specs=[pl.BlockSpec((bb, tt, D), lambda b, t: (b, t, 0)),
                  pl.BlockSpec((1, D), lambda b, t: (0, 0)),
                  pl.BlockSpec((bb, 1, D), lambda b, t: (b, 0, 0)),
                  pl.BlockSpec((bb, 1, D), lambda b, t: (b, 0, 0))],
        out_specs=pl.BlockSpec((bb, tt, D), lambda b, t: (b, t, 0)),
        out_shape=jax.ShapeDtypeStruct((B, T, D), BF16),
        compiler_params=_cparams(("parallel", "parallel")),
        name="modnorm",
    )(x, g.reshape(1, D), scale, shift)


def _mm_kernel(x_ref, w_ref, o_ref):
    bb, tt, K = x_ref.shape
    acc = jnp.dot(x_ref[...].reshape(bb * tt, K), w_ref[...], preferred_element_type=F32)
    o_ref[...] = acc.reshape(o_ref.shape).astype(o_ref.dtype)


def _mm_res_kernel(x_ref, w_ref, res_ref, gate_ref, o_ref):
    bb, tt, K = x_ref.shape
    acc = jnp.dot(x_ref[...].reshape(bb * tt, K), w_ref[...], preferred_element_type=F32)
    o_ref[...] = res_ref[...] + gate_ref[...] * acc.reshape(o_ref.shape)


def _mm(x, w, tn, out_dtype=F32, res=None, gate=None, name="mm"):
    B, T, K = x.shape
    N = w.shape[1]
    bb, tt = _row_blocks(B, T, 1024)
    in_specs = [pl.BlockSpec((bb, tt, K), lambda b, t, j: (b, t, 0)),
                pl.BlockSpec((K, tn), lambda b, t, j: (0, j))]
    args = [x, w]
    kern = _mm_kernel
    if res is not None:
        in_specs += [pl.BlockSpec((bb, tt, tn), lambda b, t, j: (b, t, j)),
                     pl.BlockSpec((bb, 1, tn), lambda b, t, j: (b, 0, j))]
        args += [res, gate]
        kern = _mm_res_kernel
    return pl.pallas_call(
        kern,
        grid=(B // bb, T // tt, N // tn),
        in_specs=in_specs,
        out_specs=pl.BlockSpec((bb, tt, tn), lambda b, t, j: (b, t, j)),
        out_shape=jax.ShapeDtypeStruct((B, T, N), out_dtype),
        compiler_params=_cparams(("parallel", "parallel", "arbitrary")),
        name=name,
    )(*args)


def _merge_kernel(hd_ref, ya_ref, yb_ref, yc_ref, yd_ref, wmu_ref, bmg_ref, wb_ref, o_ref):
    bb, tt, _ = hd_ref.shape
    rows = bb * tt
    hd = hd_ref[...].reshape(rows, MERGE_RANK).astype(BF16)
    acc = None
    for i, y_ref in enumerate((ya_ref, yb_ref, yc_ref, yd_ref)):
        gate = _sigmoid(jnp.dot(hd, wmu_ref[i], preferred_element_type=F32) + bmg_ref[i])
        term = gate * jnp.dot(y_ref[...].reshape(rows, BRANCH_W), wb_ref[i], preferred_element_type=F32)
        acc = term if acc is None else acc + term
    o_ref[...] = acc.reshape(o_ref.shape).astype(o_ref.dtype)


def _merge(p_small, ys, wmu, bmg, wb, tn=512):
    B, T, _ = p_small.shape
    D = wb.shape[2]
    bb, tt = _row_blocks(B, T, 512)
    yspec = pl.BlockSpec((bb, tt, BRANCH_W), lambda b, t, j: (b, t, 0))
    return pl.pallas_call(
        _merge_kernel,
        grid=(B // bb, T // tt, D // tn),
        in_specs=[pl.BlockSpec((bb, tt, MERGE_RANK), lambda b, t, j: (b, t, 1)),
                  yspec, yspec, yspec, yspec,
                  pl.BlockSpec((4, MERGE_RANK, tn), lambda b, t, j: (0, 0, j)),
                  pl.BlockSpec((4, 1, tn), lambda b, t, j: (0, 0, j)),
                  pl.BlockSpec((4, BRANCH_W, tn), lambda b, t, j: (0, 0, j))],
        out_specs=pl.BlockSpec((bb, tt, tn), lambda b, t, j: (b, t, j)),
        out_shape=jax.ShapeDtypeStruct((B, T, D), BF16),
        compiler_params=_cparams(("parallel", "parallel", "arbitrary")),
        name="merge",
    )(p_small, *ys, wmu, bmg, wb)


def _route_kernel(x_ref, g_ref, sc_ref, sh_ref, wh_ref, wl_ref, br_ref, h_ref, r_ref):
    bb, tt, D = x_ref.shape
    rows = bb * tt
    h = _modnorm(x_ref[...], g_ref[...], sc_ref[...], sh_ref[...])
    h_ref[...] = h.astype(h_ref.dtype)
    h = h.reshape(rows, D)
    hh, hl = _split2(h)
    f = lambda u, v: lax.dot_general(u, v, NT_DIMS, preferred_element_type=F32)
    lg = f(wh_ref[...], hh) + f(wh_ref[...], hl) + f(wl_ref[...], hh) + br_ref[...]
    gl = [lg[i:i + 1, :] for i in range(N_GROUPS)]
    el = [lg[N_GROUPS + i:N_GROUPS + i + 1, :] for i in range(N_EXPERTS)]
    best = gl[0]
    gidx = jnp.zeros(best.shape, jnp.int32)
    for i in range(1, N_GROUPS):
        m = gl[i] > best
        best = jnp.where(m, gl[i], best)
        gidx = jnp.where(m, i, gidx)
    den = None
    for i in range(N_GROUPS):
        e = jnp.exp(gl[i] - best)
        den = e if den is None else den + e
    p_grp = 1.0 / den
    e_in = []
    for j in range(EXP_PER_GROUP):
        v = el[j]
        for g in range(1, N_GROUPS):
            v = jnp.where(gidx == g, el[g * EXP_PER_GROUP + j], v)
        e_in.append(v)
    t1 = e_in[0]
    i1 = jnp.zeros(best.shape, jnp.int32)
    for j in range(1, EXP_PER_GROUP):
        m = e_in[j] > t1
        t1 = jnp.where(m, e_in[j], t1)
        i1 = jnp.where(m, j, i1)
    cand = [jnp.where(i1 == j, -jnp.inf, e_in[j]) for j in range(EXP_PER_GROUP)]
    t2 = cand[0]
    i2 = jnp.zeros(best.shape, jnp.int32)
    for j in range(1, EXP_PER_GROUP):
        m = cand[j] > t2
        t2 = jnp.where(m, cand[j], t2)
        i2 = jnp.where(m, j, i2)
    e2 = jnp.exp(t2 - t1)
    w1 = p_grp / (1.0 + e2)
    w2 = p_grp * e2 / (1.0 + e2)
    r_ref[0:1, :] = (gidx * EXP_PER_GROUP + i1).astype(F32)
    r_ref[1:2, :] = (gidx * EXP_PER_GROUP + i2).astype(F32)
    r_ref[2:3, :] = w1
    r_ref[3:4, :] = w2
    r_ref[4:8, :] = jnp.zeros((4, rows), F32)


def _route(x, g, scale, shift, wr_hi, wr_lo, br):
    B, T, D = x.shape
    bb, tt = _row_blocks(B, T, 512)
    nT = T // tt
    rows = bb * tt
    return pl.pallas_call(
        _route_kernel,
        grid=(B // bb, nT),
        in_specs=[pl.BlockSpec((bb, tt, D), lambda b, t: (b, t, 0)),
                  pl.BlockSpec((1, D), lambda b, t: (0, 0)),
                  pl.BlockSpec((bb, 1, D), lambda b, t: (b, 0, 0)),
                  pl.BlockSpec((bb, 1, D), lambda b, t: (b, 0, 0)),
                  pl.BlockSpec((128, D), lambda b, t: (0, 0)),
                  pl.BlockSpec((128, D), lambda b, t: (0, 0)),
                  pl.BlockSpec((128, 1), lambda b, t: (0, 0))],
        out_specs=[pl.BlockSpec((bb, tt, D), lambda b, t: (b, t, 0)),
                   pl.BlockSpec((8, rows), lambda b, t: (0, b * nT + t))],
        out_shape=[jax.ShapeDtypeStruct((B, T, D), F32),
                   jax.ShapeDtypeStruct((8, B * T), F32)],
        compiler_params=_cparams(("parallel", "parallel")),
        name="route",
    )(x, g.reshape(1, D), scale, shift, wr_hi, wr_lo, br)


def _row_copy(src_hbm, dst_vmem, sem, src_row, dst_row):
    return pltpu.make_async_copy(src_hbm.at[pl.ds(src_row, 1), :], dst_vmem.at[pl.ds(dst_row, 1), :], sem)


def _experts_kernel(te_ref, tv_ref, tok_ref, h_hbm, wg_ref, wu_ref, wd_ref, o_hbm,
                    stage, xbf, acc, sem_g, sem_o):
    t = pl.program_id(0)
    f = pl.program_id(1)
    nt = pl.num_programs(0)
    nf = pl.num_programs(1)
    tm = acc.shape[0]
    half = stage.shape[0]

    def gather(tile, hf, start):
        base = tile * tm + hf * half

        def body(r, c):
            cp = _row_copy(h_hbm, stage, sem_g.at[0], tok_ref[base + r], r)
            if start:
                cp.start()
            else:
                cp.wait()
            return c
        lax.fori_loop(0, half, body, 0, unroll=8)

    def writeback(tile):
        return pltpu.make_async_copy(acc, o_hbm.at[pl.ds(tile * tm, tm), :], sem_o.at[0])

    @pl.when(f == 0)
    def _():
        @pl.when(t > 0)
        def _():
            writeback(t - 1).wait()

        acc[...] = jnp.zeros(acc.shape, F32)

        @pl.when(tv_ref[t] > 0)
        def _():
            @pl.when(t == 0)
            def _():
                gather(0, 0, True)

            gather(t, 0, False)
            xbf[0:half, :] = stage[...].astype(BF16)
            gather(t, 1, True)
            gather(t, 1, False)
            xbf[half:tm, :] = stage[...].astype(BF16)
            nxt = jnp.minimum(t + 1, nt - 1)

            @pl.when((t + 1 < nt) & (tv_ref[nxt] > 0))
            def _():
                gather(nxt, 0, True)

    @pl.when(tv_ref[t] > 0)
    def _():
        x = xbf[...]
        hg = jnp.dot(x, wg_ref[...].astype(BF16), preferred_element_type=F32)
        hu = jnp.dot(x, wu_ref[...].astype(BF16), preferred_element_type=F32)
        hid = (_silu(hg) * hu).astype(BF16)
        acc[...] += jnp.dot(hid, wd_ref[...].astype(BF16), preferred_element_type=F32)

    @pl.when(f == nf - 1)
    def _():
        writeback(t).start()

        @pl.when(t == nt - 1)
        def _():
            writeback(t).wait()


def _experts(tile_expert, tile_valid, token_of_pos, h, w_gate, w_up, w_down, layer, tm, fc=128):
    Mpad = token_of_pos.shape[0]
    D = h.shape[1]
    nf = D_FF // fc
    n_tiles = Mpad // tm
    w_gate = w_gate.reshape((-1,) + w_gate.shape[2:])
    w_up = w_up.reshape((-1,) + w_up.shape[2:])
    w_down = w_down.reshape((-1,) + w_down.shape[2:])
    e0 = layer * N_EXPERTS

    def w_in_map(t, f, te, tv, tok):
        return (e0 + te[t], 0, jnp.where(tv[t] > 0, f, nf - 1))

    def w_out_map(t, f, te, tv, tok):
        return (e0 + te[t], jnp.where(tv[t] > 0, f, nf - 1), 0)

    grid_spec = pltpu.PrefetchScalarGridSpec(
        num_scalar_prefetch=3,
        grid=(n_tiles, nf),
        in_specs=[pl.BlockSpec(memory_space=pl.ANY),
                  pl.BlockSpec((None, D, fc), w_in_map),
                  pl.BlockSpec((None, D, fc), w_in_map),
                  pl.BlockSpec((None, fc, D), w_out_map)],
        out_specs=pl.BlockSpec(memory_space=pl.ANY),
        scratch_shapes=[pltpu.VMEM((tm // 2, D), F32), pltpu.VMEM((tm, D), BF16), pltpu.VMEM((tm, D), F32),
                        pltpu.SemaphoreType.DMA((1,)), pltpu.SemaphoreType.DMA((1,))],
    )
    return pl.pallas_call(
        _experts_kernel,
        grid_spec=grid_spec,
        out_shape=jax.ShapeDtypeStruct((Mpad, D), F32),
        compiler_params=_cparams(("arbitrary", "arbitrary")),
        name="experts",
    )(tile_expert, tile_valid, token_of_pos, h, w_gate, w_up, w_down)


def _combine_kernel(pos_ref, x_ref, gate_ref, w0_ref, w1_ref, gf_ref, y_hbm, o_ref, gbuf, sem, *, final):
    bb, tt, D = x_ref.shape
    rows = bb * tt
    nT = pl.num_programs(1)
    i = pl.program_id(0) * nT + pl.program_id(1)
    n = pl.num_programs(0) * nT
    M = n * rows

    def gather(tile, slot, start):
        def body(r, c):
            for k in range(2):
                cp = _row_copy(y_hbm, gbuf.at[slot, k], sem.at[slot], pos_ref[k * M + tile * rows + r], r)
                if start:
                    cp.start()
                else:
                    cp.wait()
            return c
        lax.fori_loop(0, rows, body, 0, unroll=8)

    @pl.when(i == 0)
    def _():
        gather(0, 0, True)

    for slot in range(2):
        @pl.when(i % 2 == slot)
        def _():
            @pl.when(i + 1 < n)
            def _():
                gather(i + 1, 1 - slot, True)

            gather(i, slot, False)
            g0 = gbuf[slot, 0].reshape(bb, tt, D)
            g1 = gbuf[slot, 1].reshape(bb, tt, D)
            x = x_ref[...] + gate_ref[...] * (w0_ref[...] * g0 + w1_ref[...] * g1)
            if final:
                ms = jnp.mean(x * x, axis=-1, keepdims=True)
                x = x * lax.rsqrt(ms + EPS) * gf_ref[...]
            o_ref[...] = x


def _combine(x, gate, y, pos, w0, w1, g_final=None):
    B, T, D = x.shape
    bb, tt = _row_blocks(B, T, 256)
    assert B == 1 or tt == T
    rows = bb * tt
    big = pl.BlockSpec((bb, tt, D), lambda b, t, p: (b, t, 0))
    col = pl.BlockSpec((bb, tt, 1), lambda b, t, p: (b, t, 0))
    gf = jnp.ones((1, D), F32) if g_final is None else g_final.reshape(1, D)
    grid_spec = pltpu.PrefetchScalarGridSpec(
        num_scalar_prefetch=1,
        grid=(B // bb, T // tt),
        in_specs=[big, pl.BlockSpec((bb, 1, D), lambda b, t, p: (b, 0, 0)), col, col,
                  pl.BlockSpec((1, D), lambda b, t, p: (0, 0)),
                  pl.BlockSpec(memory_space=pl.ANY)],
        out_specs=big,
        scratch_shapes=[pltpu.VMEM((2, 2, rows, D), F32), pltpu.SemaphoreType.DMA((2,))],
    )
    return pl.pallas_call(
        functools.partial(_combine_kernel, final=g_final is not None),
        grid_spec=grid_spec,
        out_shape=jax.ShapeDtypeStruct((B, T, D), F32),
        compiler_params=_cparams(("arbitrary", "arbitrary")),
        name="combine",
    )(pos, x, gate, w0, w1, gf, y)


def _moe(sets, g_norm2, wr_hi, wr_lo, br, w_e_gate, w_e_up, w_e_down, layer, g_final):
    D = sets[0][0].shape[-1]
    hs, routes = [], []
    for x1, _, scale2, shift2 in sets:
        h2, r = _route(x1, g_norm2, scale2, shift2, wr_hi, wr_lo, br)
        hs.append(h2.reshape(-1, D))
        routes.append(r)
    h_all = jnp.concatenate(hs, axis=0)
    route = jnp.concatenate(routes, axis=1)
    M = h_all.shape[0]
    tm = -(-(M // 8 + M // 96) // 32) * 32
    eid = route[0:2].astype(jnp.int32).reshape(2 * M)
    wts = route[2:4]
    onehot = (eid[:, None] == jnp.arange(N_EXPERTS, dtype=jnp.int32)[None, :]).astype(jnp.int32)
    csum = jnp.cumsum(onehot, axis=0)
    counts = csum[-1]
    rank = jnp.sum((csum - onehot) * onehot, axis=1)
    padded = ((counts + tm - 1) // tm) * tm
    starts = jnp.cumsum(padded) - padded
    pos = starts[eid] + rank
    n_tiles = -(-2 * M // tm) + N_EXPERTS
    Mpad = n_tiles * tm
    token = jnp.tile(jnp.arange(M, dtype=jnp.int32), 2)
    token_of_pos = jnp.zeros((Mpad,), jnp.int32).at[pos].set(token)
    tile_start = jnp.arange(n_tiles, dtype=jnp.int32) * tm
    ends = starts + padded
    tile_expert = jnp.sum((tile_start[:, None] >= ends[None, :]).astype(jnp.int32), axis=1)
    tile_valid = (tile_expert < N_EXPERTS).astype(jnp.int32)
    last_e = jnp.max(jnp.where(counts > 0, jnp.arange(N_EXPERTS, dtype=jnp.int32), 0))
    tile_expert = jnp.where(tile_valid > 0, tile_expert, last_e).astype(jnp.int32)
    out = _experts(tile_expert, tile_valid, token_of_pos, h_all, w_e_gate, w_e_up, w_e_down, layer, tm)
    pos = pos.astype(jnp.int32)
    outs, off = [], 0
    for x1, gate2, _, _ in sets:
        B, T, _ = x1.shape
        m = B * T
        pos_set = jnp.concatenate([pos[off:off + m], pos[M + off:M + off + m]])
        w0 = wts[0, off:off + m].reshape(B, T, 1)
        w1 = wts[1, off:off + m].reshape(B, T, 1)
        outs.append(_combine(x1, gate2, out, pos_set, w0, w1, g_final))
        off += m
    return outs


def _conv4(ext_ref, x, cw_ref, cb_ref, rows):
    ext_ref[8:8 + rows, :] = x
    out = cb_ref[...] + ext_ref[5:5 + rows, :] * cw_ref[0:1, :]
    out = out + ext_ref[6:6 + rows, :] * cw_ref[1:2, :]
    out = out + ext_ref[7:7 + rows, :] * cw_ref[2:3, :]
    return out + x * cw_ref[3:4, :]


def _lru_kernel(lx_ref, lg_ref, buf_ref, h0_ref, cw_ref, cb_ref, wa_ref, ba_ref, wx_ref, bx_ref, lam_ref,
                y_ref, nbuf_ref, hl_ref, ext_scr, a_scr, u_scr, h_scr, hc_scr):
    t = pl.program_id(1)
    Tc = lx_ref.shape[0]

    @pl.when(t == 0)
    def _():
        ext_scr[0:5, :] = jnp.zeros((5, BRANCH_W), F32)
        ext_scr[5:8, :] = buf_ref[...]
        hc_scr[...] = h0_ref[...]

    xc = _conv4(ext_scr, lx_ref[...], cw_ref, cb_ref, Tc)
    tail = ext_scr[Tc + 5:Tc + 8, :]
    ext_scr[5:8, :] = tail
    nbuf_ref[...] = tail

    def blockdiag(w_ref):
        return jnp.concatenate(
            [_dot(xc[:, n * LRU_BW:(n + 1) * LRU_BW], w_ref[n]) for n in range(LRU_BLOCKS)], axis=1)

    r = _sigmoid(blockdiag(wa_ref) + ba_ref[...])
    i = _sigmoid(blockdiag(wx_ref) + bx_ref[...])
    log_a = LRU_C * r * _log_sigmoid(lam_ref[...])
    a_scr[...] = jnp.exp(log_a)
    u_scr[...] = jnp.sqrt(-jnp.tanh(log_a) * (jnp.exp(2.0 * log_a) + 1.0)) * (i * xc)

    def body(s, h):
        h = a_scr[pl.ds(s, 1), :] * h + u_scr[pl.ds(s, 1), :]
        h_scr[pl.ds(s, 1), :] = h
        return h

    h = lax.fori_loop(0, Tc, body, hc_scr[...], unroll=8)
    hc_scr[...] = h
    hl_ref[...] = h
    y_ref[...] = (_gelu_tanh(lg_ref[...]) * h_scr[...]).astype(y_ref.dtype)


def _lru(p_lru, buf, h0, conv_w, conv_b, wa, ba, wx, bx, lam):
    B, T, _ = p_lru.shape
    W = BRANCH_W
    Tc = min(T, 256)
    row = lambda a: a.reshape(1, W)
    const = lambda shape: pl.BlockSpec(shape, lambda b, t: (0,) * len(shape))
    y, nbuf, hl = pl.pallas_call(
        _lru_kernel,
        grid=(B, T // Tc),
        in_specs=[pl.BlockSpec((None, Tc, W), lambda b, t: (b, t, 0)),
                  pl.BlockSpec((None, Tc, W), lambda b, t: (b, t, 1)),
                  pl.BlockSpec((None, CONV_W - 1, W), lambda b, t: (b, 0, 0)),
                  pl.BlockSpec((None, 1, W), lambda b, t: (b, 0, 0)),
                  const((CONV_W, W)), const((1, W)),
                  const((LRU_BLOCKS, LRU_BW, LRU_BW)), const((1, W)),
                  const((LRU_BLOCKS, LRU_BW, LRU_BW)), const((1, W)), const((1, W))],
        out_specs=[pl.BlockSpec((None, Tc, W), lambda b, t: (b, t, 0)),
                   pl.BlockSpec((None, CONV_W - 1, W), lambda b, t: (b, 0, 0)),
                   pl.BlockSpec((None, 1, W), lambda b, t: (b, 0, 0))],
        out_shape=[jax.ShapeDtypeStruct((B, T, W), BF16),
                   jax.ShapeDtypeStruct((B, CONV_W - 1, W), F32),
                   jax.ShapeDtypeStruct((B, 1, W), F32)],
        scratch_shapes=[pltpu.VMEM((Tc + 8, W), F32), pltpu.VMEM((Tc, W), F32), pltpu.VMEM((Tc, W), F32),
                        pltpu.VMEM((Tc, W), F32), pltpu.VMEM((1, W), F32)],
        compiler_params=_cparams(("parallel", "arbitrary")),
        name="rglru",
    )(p_lru, p_lru, buf, h0.reshape(B, 1, W), conv_w, row(conv_b), wa, row(ba), wx, row(bx), row(lam))
    return y, nbuf, hl.reshape(B, W)


def _gla_kernel(q_ref, k_ref, v_ref, g_ref, ga_ref, s0_ref, wup_ref, wupt_ref, b_ref, bt_ref, gn_ref,
                y_ref, s_ref, s_scr):
    t = pl.program_id(1)
    nT = pl.num_programs(1)
    Tv = q_ref.shape[0]
    C = max(Tv, CHUNK)

    @pl.when(t == 0)
    def _():
        s_scr[...] = s0_ref[...]

    q = _pad_rows(q_ref[...], C) * (GLA_DK ** -0.5)
    k = _pad_rows(k_ref[...], C)
    v = _pad_rows(v_ref[...], C)
    ga = _pad_rows(ga_ref[...], C)
    la =_log_sigmoid(_dot_hl(ga, wup_ref[...]) + b_ref[...]) / GLA_NORMALIZER
    lat = _log_sigmoid(_dot_hl_nt(wupt_ref[...], ga) + bt_ref[...]) / GLA_NORMALIZER
    if Tv < C:
        la = jnp.where(lax.broadcasted_iota(jnp.int32, la.shape, 0) < Tv, la, 0.0)
        lat = jnp.where(lax.broadcasted_iota(jnp.int32, lat.shape, 1) < Tv, lat, 0.0)
    tri = jnp.where(_tri(C), 1.0, 0.0).astype(BF16)
    bc = _dot_3x(tri, la)
    b_last = bc[C - 1:C, :]
    b_mid = bc[C // 2:C // 2 + 1, :]
    b_last_col = jnp.sum(lat, axis=1, keepdims=True)
    q_in = q * jnp.exp(bc)
    q_att = q * jnp.exp(bc - b_mid)
    k_att = k * jnp.exp(b_mid - bc)
    k_dec = k * jnp.exp(b_last - bc)
    causal = _tri(C)
    gn = gn_ref[...]
    outs = []
    for h in range(GLA_HEADS):
        ks = slice(h * GLA_DK, (h + 1) * GLA_DK)
        vs = slice(h * GLA_DV, (h + 1) * GLA_DV)
        vh = v[:, vs]
        S = s_scr[h]
        att = jnp.where(causal, _dot_nt(q_att[:, ks], k_att[:, ks]), 0.0)
        o = _dot(att, vh) + _dot(q_in[:, ks], S)
        s_scr[h] = S * jnp.exp(b_last_col[ks, :]) + _dot_tn(k_dec[:, ks], vh)
        o = o * lax.rsqrt(jnp.mean(o * o, axis=-1, keepdims=True) + EPS) * gn
        outs.append(o)
    o = jnp.concatenate(outs, axis=1)[0:Tv]
    y_ref[...] = (o * _silu(g_ref[...])).astype(y_ref.dtype)

    @pl.when(t == nT - 1)
    def _():
        s_ref[...] = s_scr[...]


def _gla(p_gla, p_small, S0, w_up, b_a, g_norm):
    B, T, _ = p_gla.shape
    Tv = min(T, CHUNK)
    const = lambda shape: pl.BlockSpec(shape, lambda b, t: (0,) * len(shape))
    st = (GLA_HEADS, GLA_DK, GLA_DV)
    w_up_pad = jnp.concatenate([w_up, jnp.zeros((128 - GLA_RANK, GLA_QK), F32)], axis=0)
    y, S = pl.pallas_call(
        _gla_kernel,
        grid=(B, T // Tv),
        in_specs=[pl.BlockSpec((None, Tv, GLA_QK), lambda b, t: (b, t, 0)),
                  pl.BlockSpec((None, Tv, GLA_QK), lambda b, t: (b, t, 1)),
                  pl.BlockSpec((None, Tv, BRANCH_W), lambda b, t: (b, t, 1)),
                  pl.BlockSpec((None, Tv, BRANCH_W), lambda b, t: (b, t, 2)),
                  pl.BlockSpec((None, Tv, 128), lambda b, t: (b, t, 0)),
                  pl.BlockSpec((None,) + st, lambda b, t: (b, 0, 0, 0)),
                  const((128, GLA_QK)), const((GLA_QK, 128)),
                  const((1, GLA_QK)), const((GLA_QK, 1)), const((1, GLA_DV))],
        out_specs=[pl.BlockSpec((None, Tv, BRANCH_W), lambda b, t: (b, t, 0)),
                   pl.BlockSpec((None,) + st, lambda b, t: (b, 0, 0, 0))],
        out_shape=[jax.ShapeDtypeStruct((B, T, BRANCH_W), BF16),
                   jax.ShapeDtypeStruct((B,) + st, F32)],
        scratch_shapes=[pltpu.VMEM(st, F32)],
        compiler_params=_cparams(("parallel", "arbitrary")),
        name="gla",
    )(p_gla, p_gla, p_gla, p_gla, p_small, S0, w_up_pad, w_up_pad.T, b_a.reshape(1, GLA_QK),
      b_a.reshape(GLA_QK, 1), g_norm.reshape(1, GLA_DV))
    return y, S


def _mlstm_kernel(qk_ref, v_ref, o_ref, if_ref, ift_ref, buf_ref, c0_ref, n0_ref, m0_ref,
                  cw_ref, cb_ref, bif_ref, bift_ref, ng_ref,
                  y_ref, nbuf_ref, c_ref, n_ref, m_ref, ext_scr, c_scr, n_scr, m_scr):
    t = pl.program_id(1)
    nT = pl.num_programs(1)
    Tv = qk_ref.shape[0]
    C = max(Tv, CHUNK)
    W2 = 2 * BRANCH_W

    @pl.when(t == 0)
    def _():
        ext_scr[0:5, :] = jnp.zeros((5, W2), F32)
        ext_scr[5:8, :] = buf_ref[...]
        c_scr[...] = c0_ref[...]
        n_scr[...] = n0_ref[...]
        m_scr[...] = jnp.zeros(m_scr.shape, F32)
        m_scr[0:1, 0:ML_HEADS] = m0_ref[...]

    qk = _conv4(ext_scr, qk_ref[...], cw_ref, cb_ref, Tv)
    tail = ext_scr[Tv + 5:Tv + 8, :]
    ext_scr[5:8, :] = tail
    nbuf_ref[...] = tail
    qk = _pad_rows(_silu(qk), C)
    q = qk[:, 0:BRANCH_W]
    k = qk[:, BRANCH_W:W2] * (ML_DH ** -0.5)
    v = _pad_rows(v_ref[...], C)

    gates = _pad_rows(if_ref[...], C) + bif_ref[...]
    i_col = gates
    lf_col = _log_sigmoid(gates)
    gates_t = _pad_rows(ift_ref[...] + bift_ref[...], 16)
    if Tv < C:
        gates_t = jnp.concatenate([gates_t, jnp.zeros((16, C - Tv), F32)], axis=1)
    i_row = gates_t
    lf_row = _log_sigmoid(gates_t)
    if Tv < C:
        rmask = lax.broadcasted_iota(jnp.int32, (C, 128), 0) < Tv
        cmask = lax.broadcasted_iota(jnp.int32, (16, C), 1) < Tv
        i_col = jnp.where(rmask, i_col, NEG_BIG)
        lf_col = jnp.where(rmask, lf_col, 0.0)
        i_row = jnp.where(cmask, i_row, NEG_BIG)
        lf_row = jnp.where(cmask, lf_row, 0.0)
    causal = _tri(C)
    tri = jnp.where(causal, 1.0, 0.0).astype(BF16)
    triu = jnp.where(_tri(C, upper=True), 1.0, 0.0).astype(BF16)
    F_col = _dot_3x(tri, lf_col)
    F_row = _dot_x3(lf_row, triu)
    ng = ng_ref[...]
    outs = []
    for h in range(ML_HEADS):
        hs = slice(h * ML_DH, (h + 1) * ML_DH)
        qh, kh, vh = q[:, hs], k[:, hs], v[:, hs]
        Fc = F_col[:, 16 + ML_HEADS + h:17 + ML_HEADS + h]
        ic = i_col[:, 16 + h:17 + h]
        m_prev = m_scr[0:1, h:h + 1]
        Fr = F_row[ML_HEADS + h:ML_HEADS + h + 1, :]
        log_d = jnp.where(causal, Fc - Fr + i_row[h:h + 1, :], NEG_BIG)
        log_inter = Fc + m_prev
        m_t = jnp.maximum(log_inter, jnp.max(log_d, axis=1, keepdims=True))
        s = _dot_nt(qh, kh) * jnp.exp(log_d - m_t)
        inter = jnp.exp(log_inter - m_t)
        Ch = c_scr[h]
        nh = n_scr[h:h + 1, :]
        num = _dot(s, vh) + inter * _dot(qh, Ch)
        den = jnp.sum(s, axis=1, keepdims=True) + inter * jnp.sum(qh * nh, axis=1, keepdims=True)
        hh = num / jnp.maximum(jnp.abs(den), jnp.exp(-m_t))
        m_new = m_t[C - 1:C, :]
        F_last = Fc[C - 1:C, :]
        decay = jnp.exp(F_last + m_prev - m_new)
        w_s = jnp.exp(F_last - Fc + ic - m_new)
        kw = kh * w_s
        c_scr[h] = decay * Ch + _dot_tn(kw, vh)
        n_scr[h:h + 1, :] = decay * nh + jnp.sum(kw, axis=0, keepdims=True)
        m_scr[0:1, h:h + 1] = m_new
        hh = hh * lax.rsqrt(jnp.mean(hh * hh, axis=-1, keepdims=True) + EPS) * ng[:, hs]
        outs.append(hh)
    hcat = jnp.concatenate(outs, axis=1)[0:Tv]
    y_ref[...] = (hcat * _sigmoid(o_ref[...])).astype(y_ref.dtype)

    @pl.when(t == nT - 1)
    def _():
        c_ref[...] = c_scr[...]
        n_ref[...] = n_scr[...]
        m_ref[...] = m_scr[0:1, 0:ML_HEADS]


def _mlstm(p_ml, p_small, buf, C0, n0, m0, conv_w, conv_b, b_if, norm_g):
    B, T, _ = p_ml.shape
    W = BRANCH_W
    Tv = min(T, CHUNK)
    ift = jnp.swapaxes(p_small[:, :, 16:16 + 2 * ML_HEADS], 1, 2)
    const = lambda shape: pl.BlockSpec(shape, lambda b, t: (0,) * len(shape))
    cst = (ML_HEADS, ML_DH, ML_DH)
    y, nbuf, Cn, nn, mn = pl.pallas_call(
        _mlstm_kernel,
        grid=(B, T // Tv),
        in_specs=[pl.BlockSpec((None, Tv, 2 * W), lambda b, t: (b, t, 0)),
                  pl.BlockSpec((None, Tv, W), lambda b, t: (b, t, 2)),
                  pl.BlockSpec((None, Tv, W), lambda b, t: (b, t, 3)),
                  pl.BlockSpec((None, Tv, 128), lambda b, t: (b, t, 0)),
                  pl.BlockSpec((None, 2 * ML_HEADS, Tv), lambda b, t: (b, 0, t)),
                  pl.BlockSpec((None, CONV_W - 1, 2 * W), lambda b, t: (b, 0, 0)),
                  pl.BlockSpec((None,) + cst, lambda b, t: (b, 0, 0, 0)),
                  pl.BlockSpec((None, ML_HEADS, ML_DH), lambda b, t: (b, 0, 0)),
                  pl.BlockSpec((None, 1, ML_HEADS), lambda b, t: (b, 0, 0)),
                  const((CONV_W, 2 * W)), const((1, 2 * W)),
                  const((1, 128)), const((2 * ML_HEADS, 1)), const((1, W))],
        out_specs=[pl.BlockSpec((None, Tv, W), lambda b, t: (b, t, 0)),
                   pl.BlockSpec((None, CONV_W - 1, 2 * W), lambda b, t: (b, 0, 0)),
                   pl.BlockSpec((None,) + cst, lambda b, t: (b, 0, 0, 0)),
                   pl.BlockSpec((None, ML_HEADS, ML_DH), lambda b, t: (b, 0, 0)),
                   pl.BlockSpec((None, 1, ML_HEADS), lambda b, t: (b, 0, 0))],
        out_shape=[jax.ShapeDtypeStruct((B, T, W), BF16),
                   jax.ShapeDtypeStruct((B, CONV_W - 1, 2 * W), F32),
                   jax.ShapeDtypeStruct((B,) + cst, F32),
                   jax.ShapeDtypeStruct((B, ML_HEADS, ML_DH), F32),
                   jax.ShapeDtypeStruct((B, 1, ML_HEADS), F32)],
        scratch_shapes=[pltpu.VMEM((Tv + 8, 2 * W), F32), pltpu.VMEM(cst, F32),
                        pltpu.VMEM((ML_HEADS, ML_DH), F32), pltpu.VMEM((8, 128), F32)],
        compiler_params=_cparams(("parallel", "arbitrary")),
        name="mlstm",
    )(p_ml, p_ml, p_ml, p_small, ift, buf, C0, n0, m0.reshape(B, 1, ML_HEADS),
      conv_w, conv_b.reshape(1, 2 * W),
      jnp.zeros((1, 128), F32).at[0, 16:16 + 2 * ML_HEADS].set(b_if), b_if.reshape(2 * ML_HEADS, 1),
      norm_g.reshape(1, W))
    return y, nbuf, Cn, nn, mn.reshape(B, ML_HEADS)


def _rwkv_kernel(p_ref, sh_ref, s0_ref, mu_ref, w0_ref, wup_ref, a0_ref, aup_ref, gup_ref,
                 kk_ref, ka_ref, rk_ref, lng_ref, lnb_ref,
                 y_ref, shn_ref, s_ref,
                 ext_scr, nkk_scr, w_scr, kka_scr, k2_scr, x_scr, v_scr, g_scr, bon_scr, yv_scr,
                 st_scr, yr_scr):
    t = pl.program_id(1)
    nT = pl.num_programs(1)
    Tb = p_ref.shape[0]
    W = BRANCH_W

    @pl.when(t == 0)
    def _():
        ext_scr[0:7, :] = jnp.zeros((7, RW_NCOLS), F32)
        ext_scr[7:8, :] = sh_ref[...]
        st_scr[...] = s0_ref[...]

    ones_blk = _seg_ones(256, RW_DH)
    p = p_ref[...]
    ext_scr[8:8 + Tb, :] = p
    prev = ext_scr[7:7 + Tb, :]
    last = p[Tb - 1:Tb, :]
    ext_scr[7:8, :] = last
    shn_ref[...] = last
    pm = p + (prev - p) * mu_ref[...]
    r = pm[:, 0:W]
    k = pm[:, W:2 * W]
    v = pm[:, 2 * W:3 * W]
    wa_d = pm[:, 3 * W:3 * W + 128]
    gd = pm[:, 3 * W + 128:3 * W + 256]
    log_w = -RW_DECAY_SCALE * _sigmoid(w0_ref[...] + _dot(jnp.tanh(wa_d), wup_ref[...]))
    a = _sigmoid(a0_ref[...] + _dot(wa_d, aup_ref[...]))
    g_scr[...] = _dot(_sigmoid(gd), gup_ref[...])
    kkr = k * kk_ref[...]
    kk = kkr / jnp.maximum(jnp.sqrt(_segsum3(kkr * kkr, ones_blk)), 1e-12)
    k2 = k * (1.0 + (a - 1.0) * ka_ref[...])
    bon_scr[...] = _segsum3(r * k2 * rk_ref[...], ones_blk) * v
    w = jnp.exp(log_w)
    kka = kk * a
    c1 = _segsum3(kka * r, ones_blk)
    yv_scr[...] = _segsum3(k2 * r, ones_blk) * v
    x_scr[...] = w * r - kk * c1
    nkk_scr[...] = -kk
    w_scr[...] = w
    kka_scr[...] = kka
    k2_scr[...] = k2
    v_scr[...] = v

    BL = 256
    eye_t = (lax.broadcasted_iota(jnp.int32, (RW_DH, BL), 0)
             == lax.broadcasted_iota(jnp.int32, (RW_DH, BL), 1) % RW_DH)
    eye_f = jnp.where(eye_t, 1.0, 0.0)
    eye_b = eye_f.astype(BF16)
    NG = W // BL

    def step(row, c):
        r1 = pl.ds(row, 1)
        S = [st_scr[:, pl.ds(gi * BL, BL)] for gi in range(NG)]
        for half in range(2):
            gis = list(range(half * NG // 2, (half + 1) * NG // 2))
            lss = [pl.ds(gi * BL, BL) for gi in gis]
            lhs = ([(S[gi] * nkk_scr[r1, ls]).astype(BF16) for gi, ls in zip(gis, lss)]
                   + [eye_b * v_scr[r1, ls].astype(BF16) for ls in lss]
                   + [(S[gi] * x_scr[r1, ls]).astype(BF16) for gi, ls in zip(gis, lss)])
            res = jnp.dot(jnp.concatenate(lhs, axis=0), ones_blk, preferred_element_type=F32)
            n = len(gis)
            for j, gi in enumerate(gis):
                ls = lss[j]
                sa = res[j * RW_DH:(j + 1) * RW_DH]
                vc = res[(n + j) * RW_DH:(n + j + 1) * RW_DH]
                yq = res[(2 * n + j) * RW_DH:(2 * n + j + 1) * RW_DH]
                st_scr[:, ls] = S[gi] * w_scr[r1, ls] + sa * kka_scr[r1, ls] + vc * k2_scr[r1, ls]
                yr_scr[r1, ls] = jnp.sum(yq * eye_f, axis=0, keepdims=True)
        return c

    lax.fori_loop(0, Tb, step, 0, unroll=4)

    y = yr_scr[...] + yv_scr[...]
    mean = _segsum3(y, ones_blk) * (1.0 / RW_DH)
    dlt = y - mean
    var = _segsum3(dlt * dlt, ones_blk) * (1.0 / RW_DH)
    yn = dlt * lax.rsqrt(var + RW_LN_EPS) * lng_ref[...] + lnb_ref[...]
    y_ref[...] = ((yn + bon_scr[...]) * g_scr[...]).astype(y_ref.dtype)

    @pl.when(t == nT - 1)
    def _():
        s_ref[...] = st_scr[...]


def _rwkv(p_rw, shift_prev, S0, mu, w0, w_up, a0, a_up, g_up, k_k, k_a, r_k, ln_g, ln_b):
    B, T, _ = p_rw.shape
    W = BRANCH_W
    Tb = min(T, 256)
    s0 = jnp.transpose(S0, (0, 2, 1, 3)).reshape(B, RW_DH, W)
    row = lambda a: a.reshape(1, -1)
    const = lambda shape: pl.BlockSpec(shape, lambda b, t: (0,) * len(shape))
    rows = lambda: pltpu.VMEM((Tb, W), F32)
    y, shn, S = pl.pallas_call(
        _rwkv_kernel,
        grid=(B, T // Tb),
        in_specs=[pl.BlockSpec((None, Tb, RW_NCOLS), lambda b, t: (b, t, 0)),
                  pl.BlockSpec((None, 1, RW_NCOLS), lambda b, t: (b, 0, 0)),
                  pl.BlockSpec((None, RW_DH, W), lambda b, t: (b, 0, 0)),
                  const((1, RW_NCOLS)), const((1, W)), const((128, W)), const((1, W)), const((128, W)),
                  const((128, W)), const((1, W)), const((1, W)), const((1, W)), const((1, W)), const((1, W))],
        out_specs=[pl.BlockSpec((None, Tb, W), lambda b, t: (b, t, 0)),
                   pl.BlockSpec((None, 1, RW_NCOLS), lambda b, t: (b, 0, 0)),
                   pl.BlockSpec((None, RW_DH, W), lambda b, t: (b, 0, 0))],
        out_shape=[jax.ShapeDtypeStruct((B, T, W), BF16),
                   jax.ShapeDtypeStruct((B, 1, RW_NCOLS), F32),
                   jax.ShapeDtypeStruct((B, RW_DH, W), F32)],
        scratch_shapes=[pltpu.VMEM((Tb + 8, RW_NCOLS), F32),
                        rows(), rows(), rows(), rows(), rows(), rows(), rows(), rows(), rows(),
                        pltpu.VMEM((RW_DH, W), F32), rows()],
        compiler_params=_cparams(("parallel", "arbitrary")),
        name="rwkv7",
    )(p_rw, shift_prev.reshape(B, 1, RW_NCOLS), s0, row(mu), row(w0),
      jnp.concatenate([w_up, jnp.zeros_like(w_up)], axis=0), row(a0),
      jnp.concatenate([jnp.zeros_like(a_up), a_up], axis=0), g_up,
      row(k_k), row(k_a), row(r_k), row(ln_g), row(ln_b))
    S = jnp.transpose(S.reshape(B, RW_DH, RW_HEADS, RW_DH), (0, 2, 1, 3))
    return y, shn.reshape(B, RW_NCOLS), S


def _prep_layer(l, w_in, w_mg_down, w_mg_up, b_mg, w_branch, w_out, w_route_g, b_route_g, w_route_e, b_route_e):
    W = BRANCH_W
    wi = w_in[l]
    offs = [0]
    for s in (GLA_QK, GLA_QK, W, W, GLA_RANK, RW_NCOLS, 2 * W, W, W, 2 * ML_HEADS, W, W):
        offs.append(offs[-1] + s)
    seg = lambda i: wi[:, offs[i]:offs[i + 1]]
    w_gla = jnp.concatenate([seg(0), seg(1), seg(2), seg(3)], axis=1).astype(BF16)
    w_rw = seg(5).astype(BF16)
    w_ml = jnp.concatenate([seg(6), seg(7), seg(8)], axis=1).astype(BF16)
    w_lru = jnp.concatenate([seg(10), seg(11)], axis=1).astype(BF16)
    pad = jnp.zeros((D_MODEL, 256 - GLA_RANK - 2 * ML_HEADS), F32)
    w_small = jnp.concatenate([seg(4), seg(9), pad, w_mg_down[l]], axis=1).astype(BF16)
    wmu = jnp.transpose(w_mg_up[l].reshape(MERGE_RANK, 4, D_MODEL), (1, 0, 2)).astype(BF16)
    bmg = b_mg[l].reshape(4, 1, D_MODEL)
    wr = jnp.concatenate([w_route_g[l], w_route_e[l]], axis=1).T
    wr = jnp.concatenate([wr, jnp.zeros((128 - wr.shape[0], D_MODEL), F32)], axis=0)
    wr_hi = wr.astype(BF16)
    wr_lo = (wr - wr_hi.astype(F32)).astype(BF16)
    br = jnp.concatenate([b_route_g[l], b_route_e[l], jnp.zeros((128 - 20,), F32)]).reshape(128, 1)
    return dict(w_gla=w_gla, w_rw=w_rw, w_ml=w_ml, w_lru=w_lru, w_small=w_small, wmu=wmu, bmg=bmg,
                wb=w_branch[l].astype(BF16), w_out=w_out[l].astype(BF16), wr_hi=wr_hi, wr_lo=wr_lo, br=br)


def _run_trunks(xs, mods, states, lw, prep, g_final):
    n_layers = mods[0].shape[0]
    xs = list(xs)
    new_states = [[] for _ in xs]
    for l in range(n_layers):
        sets = []
        for i, x in enumerate(xs):
            m = mods[i][l]
            x1, st = _mixing_sublayer(x, m, [s[l] for s in states[i]], lw, prep[l], l)
            new_states[i].append(st)
            sets.append((x1, m[:, 5:6, :], m[:, 4:5, :], m[:, 3:4, :]))
        pw = prep[l]
        xs = _moe(sets, lw["g_norm2"][l], pw["wr_hi"], pw["wr_lo"], pw["br"],
                  lw["w_e_gate"], lw["w_e_up"], lw["w_e_down"], l,
                  g_final if l == n_layers - 1 else None)
    return xs, [[jnp.stack([st[j] for st in ns]) for j in range(9)] for ns in new_states]


def _mixing_sublayer(x, m, st, lw, pw, l):
    gla_S, rw_S, rw_shift, ml_C, ml_n, ml_m, ml_conv, lru_h, lru_conv = st
    shift1, scale1, gate1 = [m[:, i:i + 1, :] for i in range(3)]
    h = _modnorm_call(x, lw["g_norm1"][l], scale1, shift1)
    p_gla = _mm(h, pw["w_gla"], 512, name="proj_gla")
    p_rw = _mm(h, pw["w_rw"], 256, name="proj_rw")
    p_ml = _mm(h, pw["w_ml"], 512, name="proj_ml")
    p_lru = _mm(h, pw["w_lru"], 512, name="proj_lru")
    p_small = _mm(h, pw["w_small"], 512, name="proj_small")
    y_a, gla_S = _gla(p_gla, p_small, gla_S, lw["gla_w_up"][l], lw["gla_b"][l], lw["gla_g_norm"][l])
    y_b, rw_shift, rw_S = _rwkv(p_rw, rw_shift, rw_S, lw["rw_mu"][l], lw["rw_w0"][l], lw["rw_w_up"][l],
                                lw["rw_a0"][l], lw["rw_a_up"][l], lw["rw_g_up"][l], lw["rw_k_k"][l],
                                lw["rw_k_a"][l], lw["rw_r_k"][l], lw["rw_ln_g"][l], lw["rw_ln_b"][l])
    y_c, ml_conv, ml_C, ml_n, ml_m = _mlstm(p_ml, p_small, ml_conv, ml_C, ml_n, ml_m, lw["ml_conv_w"][l],
                                            lw["ml_conv_b"][l], lw["ml_b_if"][l], lw["ml_norm_g"][l])
    y_d, lru_conv, lru_h = _lru(p_lru, lru_conv, lru_h, lw["lru_conv_w"][l], lw["lru_conv_b"][l],
                                lw["lru_wa"][l], lw["lru_ba"][l], lw["lru_wx"][l], lw["lru_bx"][l],
                                lw["lru_lambda"][l])
    merged = _merge(p_small, (y_a, y_b, y_c, y_d), pw["wmu"], pw["bmg"], pw["wb"])
    x1 = _mm(merged, pw["w_out"], 512, res=x, gate=gate1, name="out_proj")
    return x1, (gla_S, rw_S, rw_shift, ml_C, ml_n, ml_m, ml_conv, lru_h, lru_conv)


def kernel(x_prompt, x_sample, c_prompt, c_sample, state_gla_S, state_rwkv_S, state_rwkv_shift, state_mlstm_C, state_mlstm_n, state_mlstm_m, state_mlstm_conv, state_lru_h, state_lru_conv, w_ada, b_ada, g_norm1, g_norm2, w_in, gla_w_up, gla_b, gla_g_norm, rw_mu, rw_w0, rw_w_up, rw_a0, rw_a_up, rw_g_up, rw_k_k, rw_k_a, rw_r_k, rw_ln_g, rw_ln_b, ml_conv_w, ml_conv_b, ml_b_if, ml_norm_g, lru_conv_w, lru_conv_b, lru_wa, lru_ba, lru_wx, lru_bx, lru_lambda, w_branch, w_mg_down, w_mg_up, b_mg, w_out, w_route_g, b_route_g, w_route_e, b_route_e, w_e_gate, w_e_up, w_e_down, g_final):
    n_layers = w_ada.shape[0]
    Bp, Bs = x_prompt.shape[0], x_sample.shape[0]
    lw = dict(g_norm1=g_norm1, g_norm2=g_norm2, gla_w_up=gla_w_up, gla_b=gla_b, gla_g_norm=gla_g_norm,
              rw_mu=rw_mu, rw_w0=rw_w0, rw_w_up=rw_w_up, rw_a0=rw_a0, rw_a_up=rw_a_up, rw_g_up=rw_g_up,
              rw_k_k=rw_k_k, rw_k_a=rw_k_a, rw_r_k=rw_r_k, rw_ln_g=rw_ln_g, rw_ln_b=rw_ln_b,
              ml_conv_w=ml_conv_w, ml_conv_b=ml_conv_b, ml_b_if=ml_b_if, ml_norm_g=ml_norm_g,
              lru_conv_w=lru_conv_w, lru_conv_b=lru_conv_b, lru_wa=lru_wa, lru_ba=lru_ba, lru_wx=lru_wx,
              lru_bx=lru_bx, lru_lambda=lru_lambda, w_e_gate=w_e_gate, w_e_up=w_e_up, w_e_down=w_e_down)
    prep = [_prep_layer(l, w_in, w_mg_down, w_mg_up, b_mg, w_branch, w_out,
                        w_route_g, b_route_g, w_route_e, b_route_e) for l in range(n_layers)]
    nb = Bp + Bs
    rows = ((nb + 15) // 16) * 16
    c_all = jnp.concatenate([c_prompt, c_sample, jnp.zeros((rows - nb, D_MODEL), F32)], axis=0)
    mod = _ada_mod(c_all, w_ada, b_ada).reshape(n_layers, rows, 6, D_MODEL)
    mod_p = mod[:, 0:Bp]
    mod_s = mod[:, Bp:nb]

    def zeros(*s):
        return jnp.zeros((n_layers, Bp) + s, F32)

    zero_states = (zeros(GLA_HEADS, GLA_DK, GLA_DV), zeros(RW_HEADS, RW_DH, RW_DH), zeros(RW_NCOLS),
                   zeros(ML_HEADS, ML_DH, ML_DH), zeros(ML_HEADS, ML_DH), zeros(ML_HEADS),
                   zeros(CONV_W - 1, 2 * BRANCH_W), zeros(BRANCH_W), zeros(CONV_W - 1, BRANCH_W))
    sample_states = (state_gla_S, state_rwkv_S, state_rwkv_shift, state_mlstm_C, state_mlstm_n,
                     state_mlstm_m, state_mlstm_conv, state_lru_h, state_lru_conv)
    (y_prompt, y_sample), (ps, ss) = _run_trunks((x_prompt, x_sample), (mod_p, mod_s),
                                                 (zero_states, sample_states), lw, prep, g_final)
    return (y_prompt, y_sample, *ps, *ss)
```

```python
import functools

import jax
import jax.numpy as jnp
from jax import lax
from jax.experimental import pallas as pl
from jax.experimental.pallas import tpu as pltpu

F32 = jnp.float32
BF16 = jnp.bfloat16

D_MODEL = 4096
BRANCH_W = D_MODEL // 4
EPS = 1e-6
GLA_HEADS = 4
GLA_DK = 128
GLA_DV = 256
GLA_QK = GLA_HEADS * GLA_DK
GLA_RANK = 16
GLA_NORMALIZER = 16.0
RW_DH = 64
RW_HEADS = BRANCH_W // RW_DH
RW_NCOLS = 3 * BRANCH_W + 256
RW_DECAY_SCALE = 0.606531
RW_LN_EPS = 64e-5
ML_HEADS = 4
ML_DH = BRANCH_W // ML_HEADS
CONV_W = 4
LRU_BLOCKS = 8
LRU_BW = BRANCH_W // LRU_BLOCKS
LRU_C = 8.0
N_GROUPS = 4
EXP_PER_GROUP = 4
N_EXPERTS = 16
D_FF = D_MODEL // 4
MERGE_RANK = 256

VMEM_LIMIT = 56 * 1024 * 1024
CHUNK = 128
NEG_BIG = -1e30

NT_DIMS = (((1,), (1,)), ((), ()))
TN_DIMS = (((0,), (0,)), ((), ()))


def _cparams(sem):
    return pltpu.CompilerParams(dimension_semantics=sem, vmem_limit_bytes=VMEM_LIMIT)


def _dot(a, b):
    return jnp.dot(a.astype(BF16), b.astype(BF16), preferred_element_type=F32)


def _dot_nt(a, b):
    return lax.dot_general(a.astype(BF16), b.astype(BF16), NT_DIMS, preferred_element_type=F32)


def _dot_tn(a, b):
    return lax.dot_general(a.astype(BF16), b.astype(BF16), TN_DIMS, preferred_element_type=F32)


def _split3(x):
    hi = x.astype(BF16)
    r1 = x - hi.astype(F32)
    mid = r1.astype(BF16)
    lo = (r1 - mid.astype(F32)).astype(BF16)
    return hi, mid, lo


def _split2(x):
    hi = x.astype(BF16)
    lo = (x - hi.astype(F32)).astype(BF16)
    return hi, lo


def _dot_x3(a, b_exact):
    hi, mid, lo = _split3(a)
    f = lambda u: jnp.dot(u, b_exact, preferred_element_type=F32)
    return f(hi) + f(mid) + f(lo)


def _dot_3x(a_exact, b):
    hi, mid, lo = _split3(b)
    f = lambda u: jnp.dot(a_exact, u, preferred_element_type=F32)
    return f(hi) + f(mid) + f(lo)


def _dot_hl(a, b):
    ah, al = _split2(a)
    bh, bl = _split2(b)
    f = lambda u, v: jnp.dot(u, v, preferred_element_type=F32)
    return f(ah, bh) + f(al, bh) + f(ah, bl)


def _dot_hl_nt(a, b):
    ah, al = _split2(a)
    bh, bl = _split2(b)
    f = lambda u, v: lax.dot_general(u, v, NT_DIMS, preferred_element_type=F32)
    return f(ah, bh) + f(al, bh) + f(ah, bl)


def _sigmoid(x):
    return 1.0 / (1.0 + jnp.exp(-x))


def _silu(x):
    return x * _sigmoid(x)


def _log_sigmoid(x):
    return jnp.minimum(x, 0.0) - jnp.log(1.0 + jnp.exp(-jnp.abs(x)))


def _gelu_tanh(x):
    return 0.5 * x * (1.0 + jnp.tanh(0.7978845608028654 * (x + 0.044715 * x * x * x)))


def _tri(n, upper=False):
    r = lax.broadcasted_iota(jnp.int32, (n, n), 0)
    c = lax.broadcasted_iota(jnp.int32, (n, n), 1)
    m = (r <= c) if upper else (r >= c)
    return m


def _seg_ones(width, seg):
    r = lax.broadcasted_iota(jnp.int32, (width, width), 0) // seg
    c = lax.broadcasted_iota(jnp.int32, (width, width), 1) // seg
    return jnp.where(r == c, 1.0, 0.0).astype(BF16)


def _segsum_bf16(x_bf16, ones_blk):
    bw = ones_blk.shape[0]
    n = x_bf16.shape[1] // bw
    return jnp.concatenate(
        [jnp.dot(x_bf16[:, g * bw:(g + 1) * bw], ones_blk, preferred_element_type=F32) for g in range(n)],
        axis=1)


def _segsum3(x, ones_blk):
    hi, mid, lo = _split3(x)
    return _segsum_bf16(hi, ones_blk) + _segsum_bf16(mid, ones_blk) + _segsum_bf16(lo, ones_blk)


def _pad_rows(x, rows):
    if x.shape[0] == rows:
        return x
    return jnp.concatenate([x, jnp.zeros((rows - x.shape[0],) + x.shape[1:], x.dtype)], axis=0)


def _ada_kernel(c_ref, w_ref, b_ref, o_ref):
    c = c_ref[...]
    o_ref[...] = _dot(_silu(c), w_ref[...]) + b_ref[...]


def _ada_mod(c_all, w_ada, b_ada, tn=1024):
    n_layers, d, n = w_ada.shape
    rows = c_all.shape[0]
    return pl.pallas_call(
        _ada_kernel,
        grid=(n_layers, n // tn),
        in_specs=[pl.BlockSpec((rows, d), lambda l, j: (0, 0)),
                  pl.BlockSpec((None, d, tn), lambda l, j: (l, 0, j)),
                  pl.BlockSpec((None, 1, tn), lambda l, j: (l, 0, j))],
        out_specs=pl.BlockSpec((None, rows, tn), lambda l, j: (l, 0, j)),
        out_shape=jax.ShapeDtypeStruct((n_layers, rows, n), F32),
        compiler_params=_cparams(("arbitrary", "arbitrary")),
        name="ada_mod",
    )(c_all, w_ada, b_ada.reshape(n_layers, 1, n))


def _row_blocks(B, T, max_rows):
    if B == 1:
        return 1, min(T, max_rows)
    bb = max(1, min(B, max_rows // T))
    assert B % bb == 0
    return bb, T


def _modnorm(x, g, scale, shift):
    ms = jnp.mean(x * x, axis=-1, keepdims=True)
    h = x * lax.rsqrt(ms + EPS) * g
    return h * (1.0 + scale) + shift


def _modnorm_kernel(x_ref, g_ref, sc_ref, sh_ref, o_ref):
    o_ref[...] = _modnorm(x_ref[...], g_ref[...], sc_ref[...], sh_ref[...]).astype(o_ref.dtype)


def _modnorm_call(x, g, scale, shift):
    B, T, D = x.shape
    bb, tt = _row_blocks(B, T, 512)
    return pl.pallas_call(
        _modnorm_kernel,
        grid=(B // bb, T // tt),
        in_specs=[pl.BlockSpec((bb, tt, D), lambda b, t: (b, t, 0)),
                  pl.BlockSpec((1, D), lambda b, t: (0, 0)),
                  pl.BlockSpec((bb, 1, D), lambda b, t: (b, 0, 0)),
                  pl.BlockSpec((bb, 1, D), lambda b, t: (b, 0, 0))],
        out_specs=pl.BlockSpec((bb, tt, D), lambda b, t: (b, t, 0)),
        out_shape=jax.ShapeDtypeStruct((B, T, D), BF16),
        compiler_params=_cparams(("parallel", "parallel")),
        name="modnorm",
    )(x, g.reshape(1, D), scale, shift)


def _mm_kernel(x_ref, w_ref, o_ref):
    bb, tt, K = x_ref.shape
    acc = jnp.dot(x_ref[...].reshape(bb * tt, K), w_ref[...], preferred_element_type=F32)
    o_ref[...] = acc.reshape(o_ref.shape).astype(o_ref.dtype)


def _mm_res_kernel(x_ref, w_ref, res_ref, gate_ref, o_ref):
    bb, tt, K = x_ref.shape
    acc = jnp.dot(x_ref[...].reshape(bb * tt, K), w_ref[...], preferred_element_type=F32)
    o_ref[...] = res_ref[...] + gate_ref[...] * acc.reshape(o_ref.shape)


def _mm(x, w, tn, out_dtype=F32, res=None, gate=None, name="mm"):
    B, T, K = x.shape
    N = w.shape[1]
    bb, tt = _row_blocks(B, T, 1024)
    in_specs = [pl.BlockSpec((bb, tt, K), lambda b, t, j: (b, t, 0)),
                pl.BlockSpec((K, tn), lambda b, t, j: (0, j))]
    args = [x, w]
    kern = _mm_kernel
    if res is not None:
        in_specs += [pl.BlockSpec((bb, tt, tn), lambda b, t, j: (b, t, j)),
                     pl.BlockSpec((bb, 1, tn), lambda b, t, j: (b, 0, j))]
        args += [res, gate]
        kern = _mm_res_kernel
    return pl.pallas_call(
        kern,
        grid=(B // bb, T // tt, N // tn),
        in_specs=in_specs,
        out_specs=pl.BlockSpec((bb, tt, tn), lambda b, t, j: (b, t, j)),
        out_shape=jax.ShapeDtypeStruct((B, T, N), out_dtype),
        compiler_params=_cparams(("parallel", "parallel", "arbitrary")),
        name=name,
    )(*args)


def _merge_kernel(hd_ref, ya_ref, yb_ref, yc_ref, yd_ref, wmu_ref, bmg_ref, wb_ref, o_ref):
    bb, tt, _ = hd_ref.shape
    rows = bb * tt
    hd = hd_ref[...].reshape(rows, MERGE_RANK).astype(BF16)
    acc = None
    for i, y_ref in enumerate((ya_ref, yb_ref, yc_ref, yd_ref)):
        gate = _sigmoid(jnp.dot(hd, wmu_ref[i], preferred_element_type=F32) + bmg_ref[i])
        term = gate * jnp.dot(y_ref[...].reshape(rows, BRANCH_W), wb_ref[i], preferred_element_type=F32)
        acc = term if acc is None else acc + term
    o_ref[...] = acc.reshape(o_ref.shape).astype(o_ref.dtype)


def _merge(p_small, ys, wmu, bmg, wb, tn=512):
    B, T, _ = p_small.shape
    D = wb.shape[2]
    bb, tt = _row_blocks(B, T, 512)
    yspec = pl.BlockSpec((bb, tt, BRANCH_W), lambda b, t, j: (b, t, 0))
    return pl.pallas_call(
        _merge_kernel,
        grid=(B // bb, T // tt, D // tn),
        in_specs=[pl.BlockSpec((bb, tt, MERGE_RANK), lambda b, t, j: (b, t, 1)),
                  yspec, yspec, yspec, yspec,
                  pl.BlockSpec((4, MERGE_RANK, tn), lambda b, t, j: (0, 0, j)),
                  pl.BlockSpec((4, 1, tn), lambda b, t, j: (0, 0, j)),
                  pl.BlockSpec((4, BRANCH_W, tn), lambda b, t, j: (0, 0, j))],
        out_specs=pl.BlockSpec((bb, tt, tn), lambda b, t, j: (b, t, j)),
        out_shape=jax.ShapeDtypeStruct((B, T, D), BF16),
        compiler_params=_cparams(("parallel", "parallel", "arbitrary")),
        name="merge",
    )(p_small, *ys, wmu, bmg, wb)


def _route_kernel(x_ref, g_ref, sc_ref, sh_ref, wh_ref, wl_ref, br_ref, h_ref, r_ref):
    bb, tt, D = x_ref.shape
    rows = bb * tt
    h = _modnorm(x_ref[...], g_ref[...], sc_ref[...], sh_ref[...])
    h_ref[...] = h.astype(h_ref.dtype)
    h = h.reshape(rows, D)
    hh, hl = _split2(h)
    f = lambda u, v: lax.dot_general(u, v, NT_DIMS, preferred_element_type=F32)
    lg = f(wh_ref[...], hh) + f(wh_ref[...], hl) + f(wl_ref[...], hh) + br_ref[...]
    gl = [lg[i:i + 1, :] for i in range(N_GROUPS)]
    el = [lg[N_GROUPS + i:N_GROUPS + i + 1, :] for i in range(N_EXPERTS)]
    best = gl[0]
    gidx = jnp.zeros(best.shape, jnp.int32)
    for i in range(1, N_GROUPS):
        m = gl[i] > best
        best = jnp.where(m, gl[i], best)
        gidx = jnp.where(m, i, gidx)
    den = None
    for i in range(N_GROUPS):
        e = jnp.exp(gl[i] - best)
        den = e if den is None else den + e
    p_grp = 1.0 / den
    e_in = []
    for j in range(EXP_PER_GROUP):
        v = el[j]
        for g in range(1, N_GROUPS):
            v = jnp.where(gidx == g, el[g * EXP_PER_GROUP + j], v)
        e_in.append(v)
    t1 = e_in[0]
    i1 = jnp.zeros(best.shape, jnp.int32)
    for j in range(1, EXP_PER_GROUP):
        m = e_in[j] > t1
        t1 = jnp.where(m, e_in[j], t1)
        i1 = jnp.where(m, j, i1)
    cand = [jnp.where(i1 == j, -jnp.inf, e_in[j]) for j in range(EXP_PER_GROUP)]
    t2 = cand[0]
    i2 = jnp.zeros(best.shape, jnp.int32)
    for j in range(1, EXP_PER_GROUP):
        m = cand[j] > t2
        t2 = jnp.where(m, cand[j], t2)
        i2 = jnp.where(m, j, i2)
    e2 = jnp.exp(t2 - t1)
    w1 = p_grp / (1.0 + e2)
    w2 = p_grp * e2 / (1.0 + e2)
    r_ref[0:1, :] = (gidx * EXP_PER_GROUP + i1).astype(F32)
    r_ref[1:2, :] = (gidx * EXP_PER_GROUP + i2).astype(F32)
    r_ref[2:3, :] = w1
    r_ref[3:4, :] = w2
    r_ref[4:8, :] = jnp.zeros((4, rows), F32)


def _route(x, g, scale, shift, wr_hi, wr_lo, br):
    B, T, D = x.shape
    bb, tt = _row_blocks(B, T, 512)
    nT = T // tt
    rows = bb * tt
    return pl.pallas_call(
        _route_kernel,
        grid=(B // bb, nT),
        in_specs=[pl.BlockSpec((bb, tt, D), lambda b, t: (b, t, 0)),
                  pl.BlockSpec((1, D), lambda b, t: (0, 0)),
                  pl.BlockSpec((bb, 1, D), lambda b, t: (b, 0, 0)),
                  pl.BlockSpec((bb, 1, D), lambda b, t: (b, 0, 0)),
                  pl.BlockSpec((128, D), lambda b, t: (0, 0)),
                  pl.BlockSpec((128, D), lambda b, t: (0, 0)),
                  pl.BlockSpec((128, 1), lambda b, t: (0, 0))],
        out_specs=[pl.BlockSpec((bb, tt, D), lambda b, t: (b, t, 0)),
                   pl.BlockSpec((8, rows), lambda b, t: (0, b * nT + t))],
        out_shape=[jax.ShapeDtypeStruct((B, T, D), F32),
                   jax.ShapeDtypeStruct((8, B * T), F32)],
        compiler_params=_cparams(("parallel", "parallel")),
        name="route",
    )(x, g.reshape(1, D), scale, shift, wr_hi, wr_lo, br)


def _row_copy(src_hbm, dst_vmem, sem, src_row, dst_row):
    return pltpu.make_async_copy(src_hbm.at[pl.ds(src_row, 1), :], dst_vmem.at[pl.ds(dst_row, 1), :], sem)


def _experts_kernel(te_ref, tv_ref, tok_ref, h_hbm, wg_ref, wu_ref, wd_ref, o_ref,
                    stage, xbf, hid, sem_g, *, n_up):
    t = pl.program_id(0)
    f = pl.program_id(1)
    nt = pl.num_programs(0)
    tm = xbf.shape[0]
    half = stage.shape[0]
    fc = wg_ref.shape[1]

    def gather(tile, hf, start):
        base = tile * tm + hf * half

        def body(r, c):
            cp = _row_copy(h_hbm, stage, sem_g.at[0], tok_ref[base + r], r)
            if start:
                cp.start()
            else:
                cp.wait()
            return c
        lax.fori_loop(0, half, body, 0, unroll=8)

    valid = tv_ref[t] > 0

    @pl.when((f == 0) & valid)
    def _():
        @pl.when(t == 0)
        def _():
            gather(0, 0, True)

        gather(t, 0, False)
        xbf[0:half, :] = stage[...].astype(BF16)
        gather(t, 1, True)
        gather(t, 1, False)
        xbf[half:tm, :] = stage[...].astype(BF16)
        nxt = jnp.minimum(t + 1, nt - 1)

        @pl.when((t + 1 < nt) & (tv_ref[nxt] > 0))
        def _():
            gather(nxt, 0, True)

    for j in range(n_up):
        @pl.when((f == j) & valid)
        def _():
            x = xbf[...]
            hg = jnp.dot(x, wg_ref[...].astype(BF16), preferred_element_type=F32)
            hu = jnp.dot(x, wu_ref[...].astype(BF16), preferred_element_type=F32)
            hid[:, j * fc:(j + 1) * fc] = (_silu(hg) * hu).astype(BF16)

    @pl.when((f >= n_up) & valid)
    def _():
        o_ref[...] = jnp.dot(hid[...], wd_ref[...].astype(BF16), preferred_element_type=F32)

    @pl.when((f >= n_up) & jnp.logical_not(valid))
    def _():
        o_ref[...] = jnp.zeros(o_ref.shape, F32)


def _experts(tile_expert, tile_valid, token_of_pos, h, w_gate, w_up, w_down, layer, tm, fc=256, dn=512):
    Mpad = token_of_pos.shape[0]
    D = h.shape[1]
    n_up = D_FF // fc
    n_dn = D // dn
    n_tiles = Mpad // tm
    w_gate = w_gate.reshape((-1,) + w_gate.shape[2:])
    w_up = w_up.reshape((-1,) + w_up.shape[2:])
    w_down = w_down.reshape((-1,) + w_down.shape[2:])
    e0 = layer * N_EXPERTS

    def w_in_map(t, f, te, tv, tok):
        return (e0 + te[t], 0, jnp.where(tv[t] > 0, jnp.minimum(f, n_up - 1), n_up - 1))

    def w_out_map(t, f, te, tv, tok):
        return (e0 + te[t], 0, jnp.where(tv[t] > 0, jnp.maximum(f - n_up, 0), n_dn - 1))

    grid_spec = pltpu.PrefetchScalarGridSpec(
        num_scalar_prefetch=3,
        grid=(n_tiles, n_up + n_dn),
        in_specs=[pl.BlockSpec(memory_space=pl.ANY),
                  pl.BlockSpec((None, D, fc), w_in_map),
                  pl.BlockSpec((None, D, fc), w_in_map),
                  pl.BlockSpec((None, D_FF, dn), w_out_map)],
        out_specs=pl.BlockSpec((tm, dn), lambda t, f, te, tv, tok: (t, jnp.maximum(f - n_up, 0))),
        scratch_shapes=[pltpu.VMEM((tm // 2, D), F32), pltpu.VMEM((tm, D), BF16), pltpu.VMEM((tm, D_FF), BF16),
                        pltpu.SemaphoreType.DMA((1,))],
    )
    return pl.pallas_call(
        functools.partial(_experts_kernel, n_up=n_up),
        grid_spec=grid_spec,
        out_shape=jax.ShapeDtypeStruct((Mpad, D), F32),
        compiler_params=_cparams(("arbitrary", "arbitrary")),
        name="experts",
    )(tile_expert, tile_valid, token_of_pos, h, w_gate, w_up, w_down)


def _combine_kernel(pos_ref, x_ref, gate_ref, w0_ref, w1_ref, gf_ref, y_hbm, o_ref, gbuf, sem, *, final):
    bb, tt, D = x_ref.shape
    rows = bb * tt
    nT = pl.num_programs(1)
    i = pl.program_id(0) * nT + pl.program_id(1)
    n = pl.num_programs(0) * nT
    M = n * rows

    def gather(tile, slot, start):
        def body(r, c):
            for k in range(2):
                cp = _row_copy(y_hbm, gbuf.at[slot, k], sem.at[slot], pos_ref[k * M + tile * rows + r], r)
                if start:
                    cp.start()
                else:
                    cp.wait()
            return c
        lax.fori_loop(0, rows, body, 0, unroll=8)

    @pl.when(i == 0)
    def _():
        gather(0, 0, True)

    for slot in range(2):
        @pl.when(i % 2 == slot)
        def _():
            @pl.when(i + 1 < n)
            def _():
                gather(i + 1, 1 - slot, True)

            gather(i, slot, False)
            g0 = gbuf[slot, 0].reshape(bb, tt, D)
            g1 = gbuf[slot, 1].reshape(bb, tt, D)
            x = x_ref[...] + gate_ref[...] * (w0_ref[...] * g0 + w1_ref[...] * g1)
            if final:
                ms = jnp.mean(x * x, axis=-1, keepdims=True)
                x = x * lax.rsqrt(ms + EPS) * gf_ref[...]
            o_ref[...] = x


def _combine(x, gate, y, pos, w0, w1, g_final=None):
    B, T, D = x.shape
    bb, tt = _row_blocks(B, T, 256)
    assert B == 1 or tt == T
    rows = bb * tt
    big = pl.BlockSpec((bb, tt, D), lambda b, t, p: (b, t, 0))
    col = pl.BlockSpec((bb, tt, 1), lambda b, t, p: (b, t, 0))
    gf = jnp.ones((1, D), F32) if g_final is None else g_final.reshape(1, D)
    grid_spec = pltpu.PrefetchScalarGridSpec(
        num_scalar_prefetch=1,
        grid=(B // bb, T // tt),
        in_specs=[big, pl.BlockSpec((bb, 1, D), lambda b, t, p: (b, 0, 0)), col, col,
                  pl.BlockSpec((1, D), lambda b, t, p: (0, 0)),
                  pl.BlockSpec(memory_space=pl.ANY)],
        out_specs=big,
        scratch_shapes=[pltpu.VMEM((2, 2, rows, D), F32), pltpu.SemaphoreType.DMA((2,))],
    )
    return pl.pallas_call(
        functools.partial(_combine_kernel, final=g_final is not None),
        grid_spec=grid_spec,
        out_shape=jax.ShapeDtypeStruct((B, T, D), F32),
        compiler_params=_cparams(("arbitrary", "arbitrary")),
        name="combine",
    )(pos, x, gate, w0, w1, gf, y)


def _moe(sets, g_norm2, wr_hi, wr_lo, br, w_e_gate, w_e_up, w_e_down, layer, g_final):
    D = sets[0][0].shape[-1]
    hs, routes = [], []
    for x1, _, scale2, shift2 in sets:
        h2, r = _route(x1, g_norm2, scale2, shift2, wr_hi, wr_lo, br)
        hs.append(h2.reshape(-1, D))
        routes.append(r)
    h_all = jnp.concatenate(hs, axis=0)
    route = jnp.concatenate(routes, axis=1)
    M = h_all.shape[0]
    tm = -(-(M // 8 + M // 96) // 32) * 32
    eid = route[0:2].astype(jnp.int32).reshape(2 * M)
    wts = route[2:4]
    onehot = (eid[:, None] == jnp.arange(N_EXPERTS, dtype=jnp.int32)[None, :]).astype(jnp.int32)
    csum = jnp.cumsum(onehot, axis=0)
    counts = csum[-1]
    rank = jnp.sum((csum - onehot) * onehot, axis=1)
    padded = ((counts + tm - 1) // tm) * tm
    starts = jnp.cumsum(padded) - padded
    pos = starts[eid] + rank
    n_tiles = -(-2 * M // tm) + N_EXPERTS
    Mpad = n_tiles * tm
    token = jnp.tile(jnp.arange(M, dtype=jnp.int32), 2)
    token_of_pos = jnp.zeros((Mpad,), jnp.int32).at[pos].set(token)
    tile_start = jnp.arange(n_tiles, dtype=jnp.int32) * tm
    ends = starts + padded
    tile_expert = jnp.sum((tile_start[:, None] >= ends[None, :]).astype(jnp.int32), axis=1)
    tile_valid = (tile_expert < N_EXPERTS).astype(jnp.int32)
    last_e = jnp.max(jnp.where(counts > 0, jnp.arange(N_EXPERTS, dtype=jnp.int32), 0))
    tile_expert = jnp.where(tile_valid > 0, tile_expert, last_e).astype(jnp.int32)
    out = _experts(tile_expert, tile_valid, token_of_pos, h_all, w_e_gate, w_e_up, w_e_down, layer, tm)
    pos = pos.astype(jnp.int32)
    outs, off = [], 0
    for x1, gate2, _, _ in sets:
        B, T, _ = x1.shape
        m = B * T
        pos_set = jnp.concatenate([pos[off:off + m], pos[M + off:M + off + m]])
        w0 = wts[0, off:off + m].reshape(B, T, 1)
        w1 = wts[1, off:off + m].reshape(B, T, 1)
        outs.append(_combine(x1, gate2, out, pos_set, w0, w1, g_final))
        off += m
    return outs


def _conv4(ext_ref, x, cw_ref, cb_ref, rows):
    ext_ref[8:8 + rows, :] = x
    out = cb_ref[...] + ext_ref[5:5 + rows, :] * cw_ref[0:1, :]
    out = out + ext_ref[6:6 + rows, :] * cw_ref[1:2, :]
    out = out + ext_ref[7:7 + rows, :] * cw_ref[2:3, :]
    return out + x * cw_ref[3:4, :]


def _lru_kernel(lx_ref, lg_ref, buf_ref, h0_ref, cw_ref, cb_ref, wa_ref, ba_ref, wx_ref, bx_ref, lam_ref,
                y_ref, nbuf_ref, hl_ref, ext_scr, a_scr, u_scr, h_scr, hc_scr):
    t = pl.program_id(1)
    Tc = lx_ref.shape[0]

    @pl.when(t == 0)
    def _():
        ext_scr[0:5, :] = jnp.zeros((5, BRANCH_W), F32)
        ext_scr[5:8, :] = buf_ref[...]
        hc_scr[...] = h0_ref[...]

    xc = _conv4(ext_scr, lx_ref[...], cw_ref, cb_ref, Tc)
    tail = ext_scr[Tc + 5:Tc + 8, :]
    ext_scr[5:8, :] = tail
    nbuf_ref[...] = tail

    def blockdiag(w_ref):
        return jnp.concatenate(
            [_dot(xc[:, n * LRU_BW:(n + 1) * LRU_BW], w_ref[n]) for n in range(LRU_BLOCKS)], axis=1)

    r = _sigmoid(blockdiag(wa_ref) + ba_ref[...])
    i = _sigmoid(blockdiag(wx_ref) + bx_ref[...])
    log_a = LRU_C * r * _log_sigmoid(lam_ref[...])
    a_scr[...] = jnp.exp(log_a)
    u_scr[...] = jnp.sqrt(-jnp.tanh(log_a) * (jnp.exp(2.0 * log_a) + 1.0)) * (i * xc)

    def body(s, h):
        h = a_scr[pl.ds(s, 1), :] * h + u_scr[pl.ds(s, 1), :]
        h_scr[pl.ds(s, 1), :] = h
        return h

    h = lax.fori_loop(0, Tc, body, hc_scr[...], unroll=8)
    hc_scr[...] = h
    hl_ref[...] = h
    y_ref[...] = (_gelu_tanh(lg_ref[...]) * h_scr[...]).astype(y_ref.dtype)


def _lru(p_lru, buf, h0, conv_w, conv_b, wa, ba, wx, bx, lam):
    B, T, _ = p_lru.shape
    W = BRANCH_W
    Tc = min(T, 256)
    row = lambda a: a.reshape(1, W)
    const = lambda shape: pl.BlockSpec(shape, lambda b, t: (0,) * len(shape))
    y, nbuf, hl = pl.pallas_call(
        _lru_kernel,
        grid=(B, T // Tc),
        in_specs=[pl.BlockSpec((None, Tc, W), lambda b, t: (b, t, 0)),
                  pl.BlockSpec((None, Tc, W), lambda b, t: (b, t, 1)),
                  pl.BlockSpec((None, CONV_W - 1, W), lambda b, t: (b, 0, 0)),
                  pl.BlockSpec((None, 1, W), lambda b, t: (b, 0, 0)),
                  const((CONV_W, W)), const((1, W)),
                  const((LRU_BLOCKS, LRU_BW, LRU_BW)), const((1, W)),
                  const((LRU_BLOCKS, LRU_BW, LRU_BW)), const((1, W)), const((1, W))],
        out_specs=[pl.BlockSpec((None, Tc, W), lambda b, t: (b, t, 0)),
                   pl.BlockSpec((None, CONV_W - 1, W), lambda b, t: (b, 0, 0)),
                   pl.BlockSpec((None, 1, W), lambda b, t: (b, 0, 0))],
        out_shape=[jax.ShapeDtypeStruct((B, T, W), BF16),
                   jax.ShapeDtypeStruct((B, CONV_W - 1, W), F32),
                   jax.ShapeDtypeStruct((B, 1, W), F32)],
        scratch_shapes=[pltpu.VMEM((Tc + 8, W), F32), pltpu.VMEM((Tc, W), F32), pltpu.VMEM((Tc, W), F32),
                        pltpu.VMEM((Tc, W), F32), pltpu.VMEM((1, W), F32)],
        compiler_params=_cparams(("parallel", "arbitrary")),
        name="rglru",
    )(p_lru, p_lru, buf, h0.reshape(B, 1, W), conv_w, row(conv_b), wa, row(ba), wx, row(bx), row(lam))
    return y, nbuf, hl.reshape(B, W)


def _gla_kernel(q_ref, k_ref, v_ref, g_ref, ga_ref, s0_ref, wup_ref, wupt_ref, b_ref, bt_ref, gn_ref,
                y_ref, s_ref, s_scr):
    t = pl.program_id(1)
    nT = pl.num_programs(1)
    Tv = q_ref.shape[0]
    C = max(Tv, CHUNK)

    @pl.when(t == 0)
    def _():
        s_scr[...] = s0_ref[...]

    q = _pad_rows(q_ref[...], C) * (GLA_DK ** -0.5)
    k = _pad_rows(k_ref[...], C)
    v = _pad_rows(v_ref[...], C)
    ga = _pad_rows(ga_ref[...], C)
    la =_log_sigmoid(_dot_hl(ga, wup_ref[...]) + b_ref[...]) / GLA_NORMALIZER
    lat = _log_sigmoid(_dot_hl_nt(wupt_ref[...], ga) + bt_ref[...]) / GLA_NORMALIZER
    if Tv < C:
        la = jnp.where(lax.broadcasted_iota(jnp.int32, la.shape, 0) < Tv, la, 0.0)
        lat = jnp.where(lax.broadcasted_iota(jnp.int32, lat.shape, 1) < Tv, lat, 0.0)
    tri = jnp.where(_tri(C), 1.0, 0.0).astype(BF16)
    bc = _dot_3x(tri, la)
    b_last = bc[C - 1:C, :]
    b_mid = bc[C // 2:C // 2 + 1, :]
    b_last_col = jnp.sum(lat, axis=1, keepdims=True)
    q_in = q * jnp.exp(bc)
    q_att = q * jnp.exp(bc - b_mid)
    k_att = k * jnp.exp(b_mid - bc)
    k_dec = k * jnp.exp(b_last - bc)
    causal = _tri(C)
    gn = gn_ref[...]
    outs = []
    for h in range(GLA_HEADS):
        ks = slice(h * GLA_DK, (h + 1) * GLA_DK)
        vs = slice(h * GLA_DV, (h + 1) * GLA_DV)
        vh = v[:, vs]
        S = s_scr[h]
        att = jnp.where(causal, _dot_nt(q_att[:, ks], k_att[:, ks]), 0.0)
        o = _dot(att, vh) + _dot(q_in[:, ks], S)
        s_scr[h] = S * jnp.exp(b_last_col[ks, :]) + _dot_tn(k_dec[:, ks], vh)
        o = o * lax.rsqrt(jnp.mean(o * o, axis=-1, keepdims=True) + EPS) * gn
        outs.append(o)
    o = jnp.concatenate(outs, axis=1)[0:Tv]
    y_ref[...] = (o * _silu(g_ref[...])).astype(y_ref.dtype)

    @pl.when(t == nT - 1)
    def _():
        s_ref[...] = s_scr[...]


def _gla(p_gla, p_small, S0, w_up, b_a, g_norm):
    B, T, _ = p_gla.shape
    Tv = min(T, CHUNK)
    const = lambda shape: pl.BlockSpec(shape, lambda b, t: (0,) * len(shape))
    st = (GLA_HEADS, GLA_DK, GLA_DV)
    w_up_pad = jnp.concatenate([w_up, jnp.zeros((128 - GLA_RANK, GLA_QK), F32)], axis=0)
    y, S = pl.pallas_call(
        _gla_kernel,
        grid=(B, T // Tv),
        in_specs=[pl.BlockSpec((None, Tv, GLA_QK), lambda b, t: (b, t, 0)),
                  pl.BlockSpec((None, Tv, GLA_QK), lambda b, t: (b, t, 1)),
                  pl.BlockSpec((None, Tv, BRANCH_W), lambda b, t: (b, t, 1)),
                  pl.BlockSpec((None, Tv, BRANCH_W), lambda b, t: (b, t, 2)),
                  pl.BlockSpec((None, Tv, 128), lambda b, t: (b, t, 0)),
                  pl.BlockSpec((None,) + st, lambda b, t: (b, 0, 0, 0)),
                  const((128, GLA_QK)), const((GLA_QK, 128)),
                  const((1, GLA_QK)), const((GLA_QK, 1)), const((1, GLA_DV))],
        out_specs=[pl.BlockSpec((None, Tv, BRANCH_W), lambda b, t: (b, t, 0)),
                   pl.BlockSpec((None,) + st, lambda b, t: (b, 0, 0, 0))],
        out_shape=[jax.ShapeDtypeStruct((B, T, BRANCH_W), BF16),
                   jax.ShapeDtypeStruct((B,) + st, F32)],
        scratch_shapes=[pltpu.VMEM(st, F32)],
        compiler_params=_cparams(("parallel", "arbitrary")),
        name="gla",
    )(p_gla, p_gla, p_gla, p_gla, p_small, S0, w_up_pad, w_up_pad.T, b_a.reshape(1, GLA_QK),
      b_a.reshape(GLA_QK, 1), g_norm.reshape(1, GLA_DV))
    return y, S


def _mlstm_kernel(qk_ref, v_ref, o_ref, if_ref, ift_ref, buf_ref, c0_ref, n0_ref, m0_ref,
                  cw_ref, cb_ref, bif_ref, bift_ref, ng_ref,
                  y_ref, nbuf_ref, c_ref, n_ref, m_ref, ext_scr, c_scr, n_scr, m_scr):
    t = pl.program_id(1)
    nT = pl.num_programs(1)
    Tv = qk_ref.shape[0]
    C = max(Tv, CHUNK)
    W2 = 2 * BRANCH_W

    @pl.when(t == 0)
    def _():
        ext_scr[0:5, :] = jnp.zeros((5, W2), F32)
        ext_scr[5:8, :] = buf_ref[...]
        c_scr[...] = c0_ref[...]
        n_scr[...] = n0_ref[...]
        m_scr[...] = jnp.zeros(m_scr.shape, F32)
        m_scr[0:1, 0:ML_HEADS] = m0_ref[...]

    qk = _conv4(ext_scr, qk_ref[...], cw_ref, cb_ref, Tv)
    tail = ext_scr[Tv + 5:Tv + 8, :]
    ext_scr[5:8, :] = tail
    nbuf_ref[...] = tail
    qk = _pad_rows(_silu(qk), C)
    q = qk[:, 0:BRANCH_W]
    k = qk[:, BRANCH_W:W2] * (ML_DH ** -0.5)
    v = _pad_rows(v_ref[...], C)

    gates = _pad_rows(if_ref[...], C) + bif_ref[...]
    i_col = gates
    lf_col = _log_sigmoid(gates)
    gates_t = _pad_rows(ift_ref[...] + bift_ref[...], 16)
    if Tv < C:
        gates_t = jnp.concatenate([gates_t, jnp.zeros((16, C - Tv), F32)], axis=1)
    i_row = gates_t
    lf_row = _log_sigmoid(gates_t)
    if Tv < C:
        rmask = lax.broadcasted_iota(jnp.int32, (C, 128), 0) < Tv
        cmask = lax.broadcasted_iota(jnp.int32, (16, C), 1) < Tv
        i_col = jnp.where(rmask, i_col, NEG_BIG)
        lf_col = jnp.where(rmask, lf_col, 0.0)
        i_row = jnp.where(cmask, i_row, NEG_BIG)
        lf_row = jnp.where(cmask, lf_row, 0.0)
    causal = _tri(C)
    tri = jnp.where(causal, 1.0, 0.0).astype(BF16)
    triu = jnp.where(_tri(C, upper=True), 1.0, 0.0).astype(BF16)
    F_col = _dot_3x(tri, lf_col)
    F_row = _dot_x3(lf_row, triu)
    ng = ng_ref[...]
    outs = []
    for h in range(ML_HEADS):
        hs = slice(h * ML_DH, (h + 1) * ML_DH)
        qh, kh, vh = q[:, hs], k[:, hs], v[:, hs]
        Fc = F_col[:, 16 + ML_HEADS + h:17 + ML_HEADS + h]
        ic = i_col[:, 16 + h:17 + h]
        m_prev = m_scr[0:1, h:h + 1]
        Fr = F_row[ML_HEADS + h:ML_HEADS + h + 1, :]
        log_d = jnp.where(causal, Fc - Fr + i_row[h:h + 1, :], NEG_BIG)
        log_inter = Fc + m_prev
        m_t = jnp.maximum(log_inter, jnp.max(log_d, axis=1, keepdims=True))
        s = _dot_nt(qh, kh) * jnp.exp(log_d - m_t)
        inter = jnp.exp(log_inter - m_t)
        Ch = c_scr[h]
        nh = n_scr[h:h + 1, :]
        num = _dot(s, vh) + inter * _dot(qh, Ch)
        den = jnp.sum(s, axis=1, keepdims=True) + inter * jnp.sum(qh * nh, axis=1, keepdims=True)
        hh = num / jnp.maximum(jnp.abs(den), jnp.exp(-m_t))
        m_new = m_t[C - 1:C, :]
        F_last = Fc[C - 1:C, :]
        decay = jnp.exp(F_last + m_prev - m_new)
        w_s = jnp.exp(F_last - Fc + ic - m_new)
        kw = kh * w_s
        c_scr[h] = decay * Ch + _dot_tn(kw, vh)
        n_scr[h:h + 1, :] = decay * nh + jnp.sum(kw, axis=0, keepdims=True)
        m_scr[0:1, h:h + 1] = m_new
        hh = hh * lax.rsqrt(jnp.mean(hh * hh, axis=-1, keepdims=True) + EPS) * ng[:, hs]
        outs.append(hh)
    hcat = jnp.concatenate(outs, axis=1)[0:Tv]
    y_ref[...] = (hcat * _sigmoid(o_ref[...])).astype(y_ref.dtype)

    @pl.when(t == nT - 1)
    def _():
        c_ref[...] = c_scr[...]
        n_ref[...] = n_scr[...]
        m_ref[...] = m_scr[0:1, 0:ML_HEADS]


def _mlstm(p_ml, p_small, buf, C0, n0, m0, conv_w, conv_b, b_if, norm_g):
    B, T, _ = p_ml.shape
    W = BRANCH_W
    Tv = min(T, CHUNK)
    ift = jnp.swapaxes(p_small[:, :, 16:16 + 2 * ML_HEADS], 1, 2)
    const = lambda shape: pl.BlockSpec(shape, lambda b, t: (0,) * len(shape))
    cst = (ML_HEADS, ML_DH, ML_DH)
    y, nbuf, Cn, nn, mn = pl.pallas_call(
        _mlstm_kernel,
        grid=(B, T // Tv),
        in_specs=[pl.BlockSpec((None, Tv, 2 * W), lambda b, t: (b, t, 0)),
                  pl.BlockSpec((None, Tv, W), lambda b, t: (b, t, 2)),
                  pl.BlockSpec((None, Tv, W), lambda b, t: (b, t, 3)),
                  pl.BlockSpec((None, Tv, 128), lambda b, t: (b, t, 0)),
                  pl.BlockSpec((None, 2 * ML_HEADS, Tv), lambda b, t: (b, 0, t)),
                  pl.BlockSpec((None, CONV_W - 1, 2 * W), lambda b, t: (b, 0, 0)),
                  pl.BlockSpec((None,) + cst, lambda b, t: (b, 0, 0, 0)),
                  pl.BlockSpec((None, ML_HEADS, ML_DH), lambda b, t: (b, 0, 0)),
                  pl.BlockSpec((None, 1, ML_HEADS), lambda b, t: (b, 0, 0)),
                  const((CONV_W, 2 * W)), const((1, 2 * W)),
                  const((1, 128)), const((2 * ML_HEADS, 1)), const((1, W))],
        out_specs=[pl.BlockSpec((None, Tv, W), lambda b, t: (b, t, 0)),
                   pl.BlockSpec((None, CONV_W - 1, 2 * W), lambda b, t: (b, 0, 0)),
                   pl.BlockSpec((None,) + cst, lambda b, t: (b, 0, 0, 0)),
                   pl.BlockSpec((None, ML_HEADS, ML_DH), lambda b, t: (b, 0, 0)),
                   pl.BlockSpec((None, 1, ML_HEADS), lambda b, t: (b, 0, 0))],
        out_shape=[jax.ShapeDtypeStruct((B, T, W), BF16),
                   jax.ShapeDtypeStruct((B, CONV_W - 1, 2 * W), F32),
                   jax.ShapeDtypeStruct((B,) + cst, F32),
                   jax.ShapeDtypeStruct((B, ML_HEADS, ML_DH), F32),
                   jax.ShapeDtypeStruct((B, 1, ML_HEADS), F32)],
        scratch_shapes=[pltpu.VMEM((Tv + 8, 2 * W), F32), pltpu.VMEM(cst, F32),
                        pltpu.VMEM((ML_HEADS, ML_DH), F32), pltpu.VMEM((8, 128), F32)],
        compiler_params=_cparams(("parallel", "arbitrary")),
        name="mlstm",
    )(p_ml, p_ml, p_ml, p_small, ift, buf, C0, n0, m0.reshape(B, 1, ML_HEADS),
      conv_w, conv_b.reshape(1, 2 * W),
      jnp.zeros((1, 128), F32).at[0, 16:16 + 2 * ML_HEADS].set(b_if), b_if.reshape(2 * ML_HEADS, 1),
      norm_g.reshape(1, W))
    return y, nbuf, Cn, nn, mn.reshape(B, ML_HEADS)


def _rwkv_kernel(p_ref, sh_ref, s0_ref, mu_ref, w0_ref, wup_ref, a0_ref, aup_ref, gup_ref,
                 kk_ref, ka_ref, rk_ref, lng_ref, lnb_ref,
                 y_ref, shn_ref, s_ref,
                 ext_scr, nkk_scr, w_scr, kka_scr, k2_scr, x_scr, v_scr, g_scr, bon_scr, yv_scr,
                 st_scr, yr_scr):
    t = pl.program_id(1)
    nT = pl.num_programs(1)
    Tb = p_ref.shape[0]
    W = BRANCH_W

    @pl.when(t == 0)
    def _():
        ext_scr[0:7, :] = jnp.zeros((7, RW_NCOLS), F32)
        ext_scr[7:8, :] = sh_ref[...]
        st_scr[...] = s0_ref[...]

    ones_blk = _seg_ones(256, RW_DH)
    p = p_ref[...]
    ext_scr[8:8 + Tb, :] = p
    prev = ext_scr[7:7 + Tb, :]
    last = p[Tb - 1:Tb, :]
    ext_scr[7:8, :] = last
    shn_ref[...] = last
    pm = p + (prev - p) * mu_ref[...]
    r = pm[:, 0:W]
    k = pm[:, W:2 * W]
    v = pm[:, 2 * W:3 * W]
    wa_d = pm[:, 3 * W:3 * W + 128]
    gd = pm[:, 3 * W + 128:3 * W + 256]
    log_w = -RW_DECAY_SCALE * _sigmoid(w0_ref[...] + _dot(jnp.tanh(wa_d), wup_ref[...]))
    a = _sigmoid(a0_ref[...] + _dot(wa_d, aup_ref[...]))
    g_scr[...] = _dot(_sigmoid(gd), gup_ref[...])
    kkr = k * kk_ref[...]
    kk = kkr / jnp.maximum(jnp.sqrt(_segsum3(kkr * kkr, ones_blk)), 1e-12)
    k2 = k * (1.0 + (a - 1.0) * ka_ref[...])
    bon_scr[...] = _segsum3(r * k2 * rk_ref[...], ones_blk) * v
    w = jnp.exp(log_w)
    kka = kk * a
    c1 = _segsum3(kka * r, ones_blk)
    yv_scr[...] = _segsum3(k2 * r, ones_blk) * v
    x_scr[...] = w * r - kk * c1
    nkk_scr[...] = -kk
    w_scr[...] = w
    kka_scr[...] = kka
    k2_scr[...] = k2
    v_scr[...] = v

    BL = 256
    eye_t = (lax.broadcasted_iota(jnp.int32, (RW_DH, BL), 0)
             == lax.broadcasted_iota(jnp.int32, (RW_DH, BL), 1) % RW_DH)
    eye_f = jnp.where(eye_t, 1.0, 0.0)
    eye_b = eye_f.astype(BF16)
    NG = W // BL

    def step(row, c):
        r1 = pl.ds(row, 1)
        S = [st_scr[:, pl.ds(gi * BL, BL)] for gi in range(NG)]
        for half in range(2):
            gis = list(range(half * NG // 2, (half + 1) * NG // 2))
            lss = [pl.ds(gi * BL, BL) for gi in gis]
            lhs = ([(S[gi] * nkk_scr[r1, ls]).astype(BF16) for gi, ls in zip(gis, lss)]
                   + [eye_b * v_scr[r1, ls].astype(BF16) for ls in lss]
                   + [(S[gi] * x_scr[r1, ls]).astype(BF16) for gi, ls in zip(gis, lss)])
            res = jnp.dot(jnp.concatenate(lhs, axis=0), ones_blk, preferred_element_type=F32)
            n = len(gis)
            for j, gi in enumerate(gis):
                ls = lss[j]
                sa = res[j * RW_DH:(j + 1) * RW_DH]
                vc = res[(n + j) * RW_DH:(n + j + 1) * RW_DH]
                yq = res[(2 * n + j) * RW_DH:(2 * n + j + 1) * RW_DH]
                st_scr[:, ls] = S[gi] * w_scr[r1, ls] + sa * kka_scr[r1, ls] + vc * k2_scr[r1, ls]
                yr_scr[r1, ls] = jnp.sum(yq * eye_f, axis=0, keepdims=True)
        return c

    lax.fori_loop(0, Tb, step, 0, unroll=4)

    y = yr_scr[...] + yv_scr[...]
    mean = _segsum3(y, ones_blk) * (1.0 / RW_DH)
    dlt = y - mean
    var = _segsum3(dlt * dlt, ones_blk) * (1.0 / RW_DH)
    yn = dlt * lax.rsqrt(var + RW_LN_EPS) * lng_ref[...] + lnb_ref[...]
    y_ref[...] = ((yn + bon_scr[...]) * g_scr[...]).astype(y_ref.dtype)

    @pl.when(t == nT - 1)
    def _():
        s_ref[...] = st_scr[...]


def _rwkv(p_rw, shift_prev, S0, mu, w0, w_up, a0, a_up, g_up, k_k, k_a, r_k, ln_g, ln_b):
    B, T, _ = p_rw.shape
    W = BRANCH_W
    Tb = min(T, 256)
    s0 = jnp.transpose(S0, (0, 2, 1, 3)).reshape(B, RW_DH, W)
    row = lambda a: a.reshape(1, -1)
    const = lambda shape: pl.BlockSpec(shape, lambda b, t: (0,) * len(shape))
    rows = lambda: pltpu.VMEM((Tb, W), F32)
    y, shn, S = pl.pallas_call(
        _rwkv_kernel,
        grid=(B, T // Tb),
        in_specs=[pl.BlockSpec((None, Tb, RW_NCOLS), lambda b, t: (b, t, 0)),
                  pl.BlockSpec((None, 1, RW_NCOLS), lambda b, t: (b, 0, 0)),
                  pl.BlockSpec((None, RW_DH, W), lambda b, t: (b, 0, 0)),
                  const((1, RW_NCOLS)), const((1, W)), const((128, W)), const((1, W)), const((128, W)),
                  const((128, W)), const((1, W)), const((1, W)), const((1, W)), const((1, W)), const((1, W))],
        out_specs=[pl.BlockSpec((None, Tb, W), lambda b, t: (b, t, 0)),
                   pl.BlockSpec((None, 1, RW_NCOLS), lambda b, t: (b, 0, 0)),
                   pl.BlockSpec((None, RW_DH, W), lambda b, t: (b, 0, 0))],
        out_shape=[jax.ShapeDtypeStruct((B, T, W), BF16),
                   jax.ShapeDtypeStruct((B, 1, RW_NCOLS), F32),
                   jax.ShapeDtypeStruct((B, RW_DH, W), F32)],
        scratch_shapes=[pltpu.VMEM((Tb + 8, RW_NCOLS), F32),
                        rows(), rows(), rows(), rows(), rows(), rows(), rows(), rows(), rows(),
                        pltpu.VMEM((RW_DH, W), F32), rows()],
        compiler_params=_cparams(("parallel", "arbitrary")),
        name="rwkv7",
    )(p_rw, shift_prev.reshape(B, 1, RW_NCOLS), s0, row(mu), row(w0),
      jnp.concatenate([w_up, jnp.zeros_like(w_up)], axis=0), row(a0),
      jnp.concatenate([jnp.zeros_like(a_up), a_up], axis=0), g_up,
      row(k_k), row(k_a), row(r_k), row(ln_g), row(ln_b))
    S = jnp.transpose(S.reshape(B, RW_DH, RW_HEADS, RW_DH), (0, 2, 1, 3))
    return y, shn.reshape(B, RW_NCOLS), S


def _prep_layer(l, w_in, w_mg_down, w_mg_up, b_mg, w_branch, w_out, w_route_g, b_route_g, w_route_e, b_route_e):
    W = BRANCH_W
    wi = w_in[l]
    offs = [0]
    for s in (GLA_QK, GLA_QK, W, W, GLA_RANK, RW_NCOLS, 2 * W, W, W, 2 * ML_HEADS, W, W):
        offs.append(offs[-1] + s)
    seg = lambda i: wi[:, offs[i]:offs[i + 1]]
    w_gla = jnp.concatenate([seg(0), seg(1), seg(2), seg(3)], axis=1).astype(BF16)
    w_rw = seg(5).astype(BF16)
    w_ml = jnp.concatenate([seg(6), seg(7), seg(8)], axis=1).astype(BF16)
    w_lru = jnp.concatenate([seg(10), seg(11)], axis=1).astype(BF16)
    pad = jnp.zeros((D_MODEL, 256 - GLA_RANK - 2 * ML_HEADS), F32)
    w_small = jnp.concatenate([seg(4), seg(9), pad, w_mg_down[l]], axis=1).astype(BF16)
    wmu = jnp.transpose(w_mg_up[l].reshape(MERGE_RANK, 4, D_MODEL), (1, 0, 2)).astype(BF16)
    bmg = b_mg[l].reshape(4, 1, D_MODEL)
    wr = jnp.concatenate([w_route_g[l], w_route_e[l]], axis=1).T
    wr = jnp.concatenate([wr, jnp.zeros((128 - wr.shape[0], D_MODEL), F32)], axis=0)
    wr_hi = wr.astype(BF16)
    wr_lo = (wr - wr_hi.astype(F32)).astype(BF16)
    br = jnp.concatenate([b_route_g[l], b_route_e[l], jnp.zeros((128 - 20,), F32)]).reshape(128, 1)
    return dict(w_gla=w_gla, w_rw=w_rw, w_ml=w_ml, w_lru=w_lru, w_small=w_small, wmu=wmu, bmg=bmg,
                wb=w_branch[l].astype(BF16), w_out=w_out[l].astype(BF16), wr_hi=wr_hi, wr_lo=wr_lo, br=br)


def _run_trunks(xs, mods, states, lw, prep, g_final):
    n_layers = mods[0].shape[0]
    xs = list(xs)
    new_states = [[] for _ in xs]
    for l in range(n_layers):
        sets = []
        for i, x in enumerate(xs):
            m = mods[i][l]
            x1, st = _mixing_sublayer(x, m, [s[l] for s in states[i]], lw, prep[l], l)
            new_states[i].append(st)
            sets.append((x1, m[:, 5:6, :], m[:, 4:5, :], m[:, 3:4, :]))
        pw = prep[l]
        xs = _moe(sets, lw["g_norm2"][l], pw["wr_hi"], pw["wr_lo"], pw["br"],
                  lw["w_e_gate"], lw["w_e_up"], lw["w_e_down"], l,
                  g_final if l == n_layers - 1 else None)
    return xs, [[jnp.stack([st[j] for st in ns]) for j in range(9)] for ns in new_states]


def _mixing_sublayer(x, m, st, lw, pw, l):
    gla_S, rw_S, rw_shift, ml_C, ml_n, ml_m, ml_conv, lru_h, lru_conv = st
    shift1, scale1, gate1 = [m[:, i:i + 1, :] for i in range(3)]
    h = _modnorm_call(x, lw["g_norm1"][l], scale1, shift1)
    p_gla = _mm(h, pw["w_gla"], 512, name="proj_gla")
    p_rw = _mm(h, pw["w_rw"], 256, name="proj_rw")
    p_ml = _mm(h, pw["w_ml"], 512, name="proj_ml")
    p_lru = _mm(h, pw["w_lru"], 512, name="proj_lru")
    p_small = _mm(h, pw["w_small"], 512, name="proj_small")
    y_a, gla_S = _gla(p_gla, p_small, gla_S, lw["gla_w_up"][l], lw["gla_b"][l], lw["gla_g_norm"][l])
    y_b, rw_shift, rw_S = _rwkv(p_rw, rw_shift, rw_S, lw["rw_mu"][l], lw["rw_w0"][l], lw["rw_w_up"][l],
                                lw["rw_a0"][l], lw["rw_a_up"][l], lw["rw_g_up"][l], lw["rw_k_k"][l],
                                lw["rw_k_a"][l], lw["rw_r_k"][l], lw["rw_ln_g"][l], lw["rw_ln_b"][l])
    y_c, ml_conv, ml_C, ml_n, ml_m = _mlstm(p_ml, p_small, ml_conv, ml_C, ml_n, ml_m, lw["ml_conv_w"][l],
                                            lw["ml_conv_b"][l], lw["ml_b_if"][l], lw["ml_norm_g"][l])
    y_d, lru_conv, lru_h = _lru(p_lru, lru_conv, lru_h, lw["lru_conv_w"][l], lw["lru_conv_b"][l],
                                lw["lru_wa"][l], lw["lru_ba"][l], lw["lru_wx"][l], lw["lru_bx"][l],
                                lw["lru_lambda"][l])
    merged = _merge(p_small, (y_a, y_b, y_c, y_d), pw["wmu"], pw["bmg"], pw["wb"])
    x1 = _mm(merged, pw["w_out"], 512, res=x, gate=gate1, name="out_proj")
    return x1, (gla_S, rw_S, rw_shift, ml_C, ml_n, ml_m, ml_conv, lru_h, lru_conv)


def kernel(x_prompt, x_sample, c_prompt, c_sample, state_gla_S, state_rwkv_S, state_rwkv_shift, state_mlstm_C, state_mlstm_n, state_mlstm_m, state_mlstm_conv, state_lru_h, state_lru_conv, w_ada, b_ada, g_norm1, g_norm2, w_in, gla_w_up, gla_b, gla_g_norm, rw_mu, rw_w0, rw_w_up, rw_a0, rw_a_up, rw_g_up, rw_k_k, rw_k_a, rw_r_k, rw_ln_g, rw_ln_b, ml_conv_w, ml_conv_b, ml_b_if, ml_norm_g, lru_conv_w, lru_conv_b, lru_wa, lru_ba, lru_wx, lru_bx, lru_lambda, w_branch, w_mg_down, w_mg_up, b_mg, w_out, w_route_g, b_route_g, w_route_e, b_route_e, w_e_gate, w_e_up, w_e_down, g_final):
    n_layers = w_ada.shape[0]
    Bp, Bs = x_prompt.shape[0], x_sample.shape[0]
    lw = dict(g_norm1=g_norm1, g_norm2=g_norm2, gla_w_up=gla_w_up, gla_b=gla_b, gla_g_norm=gla_g_norm,
              rw_mu=rw_mu, rw_w0=rw_w0, rw_w_up=rw_w_up, rw_a0=rw_a0, rw_a_up=rw_a_up, rw_g_up=rw_g_up,
              rw_k_k=rw_k_k, rw_k_a=rw_k_a, rw_r_k=rw_r_k, rw_ln_g=rw_ln_g, rw_ln_b=rw_ln_b,
              ml_conv_w=ml_conv_w, ml_conv_b=ml_conv_b, ml_b_if=ml_b_if, ml_norm_g=ml_norm_g,
              lru_conv_w=lru_conv_w, lru_conv_b=lru_conv_b, lru_wa=lru_wa, lru_ba=lru_ba, lru_wx=lru_wx,
              lru_bx=lru_bx, lru_lambda=lru_lambda, w_e_gate=w_e_gate, w_e_up=w_e_up, w_e_down=w_e_down)
    prep = [_prep_layer(l, w_in, w_mg_down, w_mg_up, b_mg, w_branch, w_out,
                        w_route_g, b_route_g, w_route_e, b_route_e) for l in range(n_layers)]
    nb = Bp + Bs
    rows = ((nb + 15) // 16) * 16
    c_all = jnp.concatenate([c_prompt, c_sample, jnp.zeros((rows - nb, D_MODEL), F32)], axis=0)
    mod = _ada_mod(c_all, w_ada, b_ada).reshape(n_layers, rows, 6, D_MODEL)
    mod_p = mod[:, 0:Bp]
    mod_s = mod[:, Bp:nb]

    def zeros(*s):
        return jnp.zeros((n_layers, Bp) + s, F32)

    zero_states = (zeros(GLA_HEADS, GLA_DK, GLA_DV), zeros(RW_HEADS, RW_DH, RW_DH), zeros(RW_NCOLS),
                   zeros(ML_HEADS, ML_DH, ML_DH), zeros(ML_HEADS, ML_DH), zeros(ML_HEADS),
                   zeros(CONV_W - 1, 2 * BRANCH_W), zeros(BRANCH_W), zeros(CONV_W - 1, BRANCH_W))
    sample_states = (state_gla_S, state_rwkv_S, state_rwkv_shift, state_mlstm_C, state_mlstm_n,
                     state_mlstm_m, state_mlstm_conv, state_lru_h, state_lru_conv)
    (y_prompt, y_sample), (ps, ss) = _run_trunks((x_prompt, x_sample), (mod_p, mod_s),
                                                 (zero_states, sample_states), lw, prep, g_final)
    return (y_prompt, y_sample, *ps, *ss)
```

```python
import functools

import jax
import jax.numpy as jnp
from jax import lax
from jax.experimental import pallas as pl
from jax.experimental.pallas import tpu as pltpu

F32 = jnp.float32
BF16 = jnp.bfloat16

D_MODEL = 4096
BRANCH_W = D_MODEL // 4
EPS = 1e-6
GLA_HEADS = 4
GLA_DK = 128
GLA_DV = 256
GLA_QK = GLA_HEADS * GLA_DK
GLA_RANK = 16
GLA_NORMALIZER = 16.0
RW_DH = 64
RW_HEADS = BRANCH_W // RW_DH
RW_NCOLS = 3 * BRANCH_W + 256
RW_DECAY_SCALE = 0.606531
RW_LN_EPS = 64e-5
ML_HEADS = 4
ML_DH = BRANCH_W // ML_HEADS
CONV_W = 4
LRU_BLOCKS = 8
LRU_BW = BRANCH_W // LRU_BLOCKS
LRU_C = 8.0
N_GROUPS = 4
EXP_PER_GROUP = 4
N_EXPERTS = 16
D_FF = D_MODEL // 4
MERGE_RANK = 256

VMEM_LIMIT = 56 * 1024 * 1024
CHUNK = 128
NEG_BIG = -1e30

NT_DIMS = (((1,), (1,)), ((), ()))
TN_DIMS = (((0,), (0,)), ((), ()))


def _cparams(sem):
    return pltpu.CompilerParams(dimension_semantics=sem, vmem_limit_bytes=VMEM_LIMIT)


def _dot(a, b):
    return jnp.dot(a.astype(BF16), b.astype(BF16), preferred_element_type=F32)


def _dot_nt(a, b):
    return lax.dot_general(a.astype(BF16), b.astype(BF16), NT_DIMS, preferred_element_type=F32)


def _dot_tn(a, b):
    return lax.dot_general(a.astype(BF16), b.astype(BF16), TN_DIMS, preferred_element_type=F32)


def _split3(x):
    hi = x.astype(BF16)
    r1 = x - hi.astype(F32)
    mid = r1.astype(BF16)
    lo = (r1 - mid.astype(F32)).astype(BF16)
    return hi, mid, lo


def _split2(x):
    hi = x.astype(BF16)
    lo = (x - hi.astype(F32)).astype(BF16)
    return hi, lo


def _dot_x3(a, b_exact):
    hi, mid, lo = _split3(a)
    f = lambda u: jnp.dot(u, b_exact, preferred_element_type=F32)
    return f(hi) + f(mid) + f(lo)


def _dot_3x(a_exact, b):
    hi, mid, lo = _split3(b)
    f = lambda u: jnp.dot(a_exact, u, preferred_element_type=F32)
    return f(hi) + f(mid) + f(lo)


def _dot_hl(a, b):
    ah, al = _split2(a)
    bh, bl = _split2(b)
    f = lambda u, v: jnp.dot(u, v, preferred_element_type=F32)
    return f(ah, bh) + f(al, bh) + f(ah, bl)


def _dot_hl_nt(a, b):
    ah, al = _split2(a)
    bh, bl = _split2(b)
    f = lambda u, v: lax.dot_general(u, v, NT_DIMS, preferred_element_type=F32)
    return f(ah, bh) + f(al, bh) + f(ah, bl)


def _sigmoid(x):
    return 1.0 / (1.0 + jnp.exp(-x))


def _silu(x):
    return x * _sigmoid(x)


def _log_sigmoid(x):
    return jnp.minimum(x, 0.0) - jnp.log(1.0 + jnp.exp(-jnp.abs(x)))


def _gelu_tanh(x):
    return 0.5 * x * (1.0 + jnp.tanh(0.7978845608028654 * (x + 0.044715 * x * x * x)))


def _tri(n, upper=False):
    r = lax.broadcasted_iota(jnp.int32, (n, n), 0)
    c = lax.broadcasted_iota(jnp.int32, (n, n), 1)
    m = (r <= c) if upper else (r >= c)
    return m


def _seg_ones(width, seg):
    r = lax.broadcasted_iota(jnp.int32, (width, width), 0) // seg
    c = lax.broadcasted_iota(jnp.int32, (width, width), 1) // seg
    return jnp.where(r == c, 1.0, 0.0).astype(BF16)


def _segsum_bf16(x_bf16, ones_blk):
    bw = ones_blk.shape[0]
    n = x_bf16.shape[1] // bw
    return jnp.concatenate(
        [jnp.dot(x_bf16[:, g * bw:(g + 1) * bw], ones_blk, preferred_element_type=F32) for g in range(n)],
        axis=1)


def _segsum3(x, ones_blk):
    hi, mid, lo = _split3(x)
    return _segsum_bf16(hi, ones_blk) + _segsum_bf16(mid, ones_blk) + _segsum_bf16(lo, ones_blk)


def _pad_rows(x, rows):
    if x.shape[0] == rows:
        return x
    return jnp.concatenate([x, jnp.zeros((rows - x.shape[0],) + x.shape[1:], x.dtype)], axis=0)


def _ada_kernel(c_ref, w_ref, b_ref, o_ref):
    c = c_ref[...]
    o_ref[...] = _dot(_silu(c), w_ref[...]) + b_ref[...]


def _ada_mod(c_all, w_ada, b_ada, tn=512):
    n_layers, d, n = w_ada.shape
    rows = c_all.shape[0]
    return pl.pallas_call(
        _ada_kernel,
        grid=(n_layers, n // tn),
        in_specs=[pl.BlockSpec((rows, d), lambda l, j: (0, 0)),
                  pl.BlockSpec((None, d, tn), lambda l, j: (l, 0, j)),
                  pl.BlockSpec((None, 1, tn), lambda l, j: (l, 0, j))],
        out_specs=pl.BlockSpec((None, rows, tn), lambda l, j: (l, 0, j)),
        out_shape=jax.ShapeDtypeStruct((n_layers, rows, n), F32),
        compiler_params=_cparams(("arbitrary", "arbitrary")),
        name="ada_mod",
    )(c_all, w_ada, b_ada.reshape(n_layers, 1, n))


def _row_blocks(B, T, max_rows):
    if B == 1:
        return 1, min(T, max_rows)
    bb = max(1, min(B, max_rows // T))
    assert B % bb == 0
    return bb, T


def _modnorm(x, g, scale, shift):
    ms = jnp.mean(x * x, axis=-1, keepdims=True)
    h = x * lax.rsqrt(ms + EPS) * g
    return h * (1.0 + scale) + shift


def _modnorm_kernel(x_ref, g_ref, sc_ref, sh_ref, o_ref):
    o_ref[...] = _modnorm(x_ref[...], g_ref[...], sc_ref[...], sh_ref[...]).astype(o_ref.dtype)


def _modnorm_call(x, g, scale, shift):
    B, T, D = x.shape
    bb, tt = _row_blocks(B, T, 512)
    return pl.pallas_call(
        _modnorm_kernel,
        grid=(B // bb, T // tt),
        in_specs=[pl.BlockSpec((bb, tt, D), lambda b, t: (b, t, 0)),
                  pl.BlockSpec((1, D), lambda b, t: (0, 0)),
                  pl.BlockSpec((bb, 1, D), lambda b, t: (b, 0, 0)),
                  pl.BlockSpec((bb, 1, D), lambda b, t: (b, 0, 0))],
        out_specs=pl.BlockSpec((bb, tt, D), lambda b, t: (b, t, 0)),
        out_shape=jax.ShapeDtypeStruct((B, T, D), BF16),
        compiler_params=_cparams(("parallel", "parallel")),
        name="modnorm",
    )(x, g.reshape(1, D), scale, shift)


def _mm_kernel(x_ref, w_ref, o_ref):
    bb, tt, K = x_ref.shape
    acc = jnp.dot(x_ref[...].reshape(bb * tt, K), w_ref[...], preferred_element_type=F32)
    o_ref[...] = acc.reshape(o_ref.shape).astype(o_ref.dtype)


def _mm_res_kernel(x_ref, w_ref, res_ref, gate_ref, o_ref):
    bb, tt, K = x_ref.shape
    acc = jnp.dot(x_ref[...].reshape(bb * tt, K), w_ref[...], preferred_element_type=F32)
    o_ref[...] = res_ref[...] + gate_ref[...] * acc.reshape(o_ref.shape)


def _mm(x, w, tn, out_dtype=F32, res=None, gate=None, name="mm"):
    B, T, K = x.shape
    N = w.shape[1]
    bb, tt = _row_blocks(B, T, 1024)
    in_specs = [pl.BlockSpec((bb, tt, K), lambda b, t, j: (b, t, 0)),
                pl.BlockSpec((K, tn), lambda b, t, j: (0, j))]
    args = [x, w]
    kern = _mm_kernel
    if res is not None:
        in_specs += [pl.BlockSpec((bb, tt, tn), lambda b, t, j: (b, t, j)),
                     pl.BlockSpec((bb, 1, tn), lambda b, t, j: (b, 0, j))]
        args += [res, gate]
        kern = _mm_res_kernel
    return pl.pallas_call(
        kern,
        grid=(B // bb, T // tt, N // tn),
        in_specs=in_specs,
        out_specs=pl.BlockSpec((bb, tt, tn), lambda b, t, j: (b, t, j)),
        out_shape=jax.ShapeDtypeStruct((B, T, N), out_dtype),
        compiler_params=_cparams(("parallel", "parallel", "arbitrary")),
        name=name,
    )(*args)


def _merge_kernel(hd_ref, ya_ref, yb_ref, yc_ref, yd_ref, wmu_ref, bmg_ref, wb_ref, o_ref):
    bb, tt, _ = hd_ref.shape
    rows = bb * tt
    hd = hd_ref[...].reshape(rows, MERGE_RANK).astype(BF16)
    acc = None
    for i, y_ref in enumerate((ya_ref, yb_ref, yc_ref, yd_ref)):
        gate = _sigmoid(jnp.dot(hd, wmu_ref[i], preferred_element_type=F32) + bmg_ref[i])
        term = gate * jnp.dot(y_ref[...].reshape(rows, BRANCH_W), wb_ref[i], preferred_element_type=F32)
        acc = term if acc is None else acc + term
    o_ref[...] = acc.reshape(o_ref.shape).astype(o_ref.dtype)


def _merge(p_small, ys, wmu, bmg, wb, tn=512):
    B, T, _ = p_small.shape
    D = wb.shape[2]
    bb, tt = _row_blocks(B, T, 512)
    yspec = pl.BlockSpec((bb, tt, BRANCH_W), lambda b, t, j: (b, t, 0))
    return pl.pallas_call(
        _merge_kernel,
        grid=(B // bb, T // tt, D // tn),
        in_specs=[pl.BlockSpec((bb, tt, MERGE_RANK), lambda b, t, j: (b, t, 1)),
                  yspec, yspec, yspec, yspec,
                  pl.BlockSpec((4, MERGE_RANK, tn), lambda b, t, j: (0, 0, j)),
                  pl.BlockSpec((4, 1, tn), lambda b, t, j: (0, 0, j)),
                  pl.BlockSpec((4, BRANCH_W, tn), lambda b, t, j: (0, 0, j))],
        out_specs=pl.BlockSpec((bb, tt, tn), lambda b, t, j: (b, t, j)),
        out_shape=jax.ShapeDtypeStruct((B, T, D), BF16),
        compiler_params=_cparams(("parallel", "parallel", "arbitrary")),
        name="merge",
    )(p_small, *ys, wmu, bmg, wb)


def _route_kernel(x_ref, g_ref, sc_ref, sh_ref, wh_ref, wl_ref, br_ref, h_ref, r_ref):
    bb, tt, D = x_ref.shape
    rows = bb * tt
    h = _modnorm(x_ref[...], g_ref[...], sc_ref[...], sh_ref[...])
    h_ref[...] = h.astype(h_ref.dtype)
    h = h.reshape(rows, D)
    hh, hl = _split2(h)
    f = lambda u, v: lax.dot_general(u, v, NT_DIMS, preferred_element_type=F32)
    lg = f(wh_ref[...], hh) + f(wh_ref[...], hl) + f(wl_ref[...], hh) + br_ref[...]
    gl = [lg[i:i + 1, :] for i in range(N_GROUPS)]
    el = [lg[N_GROUPS + i:N_GROUPS + i + 1, :] for i in range(N_EXPERTS)]
    best = gl[0]
    gidx = jnp.zeros(best.shape, jnp.int32)
    for i in range(1, N_GROUPS):
        m = gl[i] > best
        best = jnp.where(m, gl[i], best)
        gidx = jnp.where(m, i, gidx)
    den = None
    for i in range(N_GROUPS):
        e = jnp.exp(gl[i] - best)
        den = e if den is None else den + e
    p_grp = 1.0 / den
    e_in = []
    for j in range(EXP_PER_GROUP):
        v = el[j]
        for g in range(1, N_GROUPS):
            v = jnp.where(gidx == g, el[g * EXP_PER_GROUP + j], v)
        e_in.append(v)
    t1 = e_in[0]
    i1 = jnp.zeros(best.shape, jnp.int32)
    for j in range(1, EXP_PER_GROUP):
        m = e_in[j] > t1
        t1 = jnp.where(m, e_in[j], t1)
        i1 = jnp.where(m, j, i1)
    cand = [jnp.where(i1 == j, -jnp.inf, e_in[j]) for j in range(EXP_PER_GROUP)]
    t2 = cand[0]
    i2 = jnp.zeros(best.shape, jnp.int32)
    for j in range(1, EXP_PER_GROUP):
        m = cand[j] > t2
        t2 = jnp.where(m, cand[j], t2)
        i2 = jnp.where(m, j, i2)
    e2 = jnp.exp(t2 - t1)
    w1 = p_grp / (1.0 + e2)
    w2 = p_grp * e2 / (1.0 + e2)
    r_ref[0:1, :] = (gidx * EXP_PER_GROUP + i1).astype(F32)
    r_ref[1:2, :] = (gidx * EXP_PER_GROUP + i2).astype(F32)
    r_ref[2:3, :] = w1
    r_ref[3:4, :] = w2
    r_ref[4:8, :] = jnp.zeros((4, rows), F32)


def _route(x, g, scale, shift, wr_hi, wr_lo, br):
    B, T, D = x.shape
    bb, tt = _row_blocks(B, T, 512)
    nT = T // tt
    rows = bb * tt
    return pl.pallas_call(
        _route_kernel,
        grid=(B // bb, nT),
        in_specs=[pl.BlockSpec((bb, tt, D), lambda b, t: (b, t, 0)),
                  pl.BlockSpec((1, D), lambda b, t: (0, 0)),
                  pl.BlockSpec((bb, 1, D), lambda b, t: (b, 0, 0)),
                  pl.BlockSpec((bb, 1, D), lambda b, t: (b, 0, 0)),
                  pl.BlockSpec((128, D), lambda b, t: (0, 0)),
                  pl.BlockSpec((128, D), lambda b, t: (0, 0)),
                  pl.BlockSpec((128, 1), lambda b, t: (0, 0))],
        out_specs=[pl.BlockSpec((bb, tt, D), lambda b, t: (b, t, 0)),
                   pl.BlockSpec((8, rows), lambda b, t: (0, b * nT + t))],
        out_shape=[jax.ShapeDtypeStruct((B, T, D), F32),
                   jax.ShapeDtypeStruct((8, B * T), F32)],
        compiler_params=_cparams(("parallel", "parallel")),
        name="route",
    )(x, g.reshape(1, D), scale, shift, wr_hi, wr_lo, br)


def _row_copy(src_hbm, dst_vmem, sem, src_row, dst_row):
    return pltpu.make_async_copy(src_hbm.at[pl.ds(src_row, 1), :], dst_vmem.at[pl.ds(dst_row, 1), :], sem)


def _experts_kernel(te_ref, tv_ref, tok_ref, h_hbm, wg_ref, wu_ref, wd_ref, o_ref, stage, xbf, sem):
    t = pl.program_id(0)
    f = pl.program_id(1)
    nt = pl.num_programs(0)
    tm = stage.shape[0]

    def gather(tile, start):
        def body(r, c):
            cp = _row_copy(h_hbm, stage, sem.at[0], tok_ref[tile * tm + r], r)
            if start:
                cp.start()
            else:
                cp.wait()
            return c
        lax.fori_loop(0, tm, body, 0, unroll=8)

    @pl.when(f == 0)
    def _():
        o_ref[...] = jnp.zeros(o_ref.shape, F32)

        @pl.when((t == 0) & (tv_ref[0] > 0))
        def _():
            gather(0, True)

        @pl.when(tv_ref[t] > 0)
        def _():
            gather(t, False)
            xbf[...] = stage[...].astype(BF16)

        nxt = jnp.minimum(t + 1, nt - 1)

        @pl.when((t + 1 < nt) & (tv_ref[nxt] > 0))
        def _():
            gather(nxt, True)

    @pl.when(tv_ref[t] > 0)
    def _():
        x = xbf[...]
        hg = jnp.dot(x, wg_ref[...].astype(BF16), preferred_element_type=F32)
        hu = jnp.dot(x, wu_ref[...].astype(BF16), preferred_element_type=F32)
        hid = (_silu(hg) * hu).astype(BF16)
        o_ref[...] += jnp.dot(hid, wd_ref[...].astype(BF16), preferred_element_type=F32)


def _experts(tile_expert, tile_valid, token_of_pos, h, w_gate, w_up, w_down, layer, tm, fc=256):
    Mpad = token_of_pos.shape[0]
    D = h.shape[1]
    nf = D_FF // fc
    n_tiles = Mpad // tm
    w_gate = w_gate.reshape((-1,) + w_gate.shape[2:])
    w_up = w_up.reshape((-1,) + w_up.shape[2:])
    w_down = w_down.reshape((-1,) + w_down.shape[2:])
    e0 = layer * N_EXPERTS

    def w_in_map(t, f, te, tv, tok):
        return (e0 + te[t], 0, jnp.where(tv[t] > 0, f, nf - 1))

    def w_out_map(t, f, te, tv, tok):
        return (e0 + te[t], jnp.where(tv[t] > 0, f, nf - 1), 0)

    grid_spec = pltpu.PrefetchScalarGridSpec(
        num_scalar_prefetch=3,
        grid=(n_tiles, nf),
        in_specs=[pl.BlockSpec(memory_space=pl.ANY),
                  pl.BlockSpec((None, D, fc), w_in_map),
                  pl.BlockSpec((None, D, fc), w_in_map),
                  pl.BlockSpec((None, fc, D), w_out_map)],
        out_specs=pl.BlockSpec((tm, D), lambda t, f, te, tv, tok: (t, 0)),
        scratch_shapes=[pltpu.VMEM((tm, D), F32), pltpu.VMEM((tm, D), BF16),
                        pltpu.SemaphoreType.DMA((1,))],
    )
    return pl.pallas_call(
        _experts_kernel,
        grid_spec=grid_spec,
        out_shape=jax.ShapeDtypeStruct((Mpad, D), F32),
        compiler_params=_cparams(("arbitrary", "arbitrary")),
        name="experts",
    )(tile_expert, tile_valid, token_of_pos, h, w_gate, w_up, w_down)


def _combine_kernel(pos_ref, x_ref, gate_ref, w0_ref, w1_ref, gf_ref, y_hbm, o_ref, gbuf, sem, *, final):
    bb, tt, D = x_ref.shape
    rows = bb * tt
    nT = pl.num_programs(1)
    i = pl.program_id(0) * nT + pl.program_id(1)
    n = pl.num_programs(0) * nT
    M = n * rows

    def gather(tile, slot, start):
        def body(r, c):
            for k in range(2):
                cp = _row_copy(y_hbm, gbuf.at[slot, k], sem.at[slot], pos_ref[k * M + tile * rows + r], r)
                if start:
                    cp.start()
                else:
                    cp.wait()
            return c
        lax.fori_loop(0, rows, body, 0, unroll=8)

    @pl.when(i == 0)
    def _():
        gather(0, 0, True)

    for slot in range(2):
        @pl.when(i % 2 == slot)
        def _():
            @pl.when(i + 1 < n)
            def _():
                gather(i + 1, 1 - slot, True)

            gather(i, slot, False)
            g0 = gbuf[slot, 0].reshape(bb, tt, D)
            g1 = gbuf[slot, 1].reshape(bb, tt, D)
            x = x_ref[...] + gate_ref[...] * (w0_ref[...] * g0 + w1_ref[...] * g1)
            if final:
                ms = jnp.mean(x * x, axis=-1, keepdims=True)
                x = x * lax.rsqrt(ms + EPS) * gf_ref[...]
            o_ref[...] = x


def _combine(x, gate, y, pos, w0, w1, g_final=None):
    B, T, D = x.shape
    bb, tt = _row_blocks(B, T, 256)
    assert B == 1 or tt == T
    rows = bb * tt
    big = pl.BlockSpec((bb, tt, D), lambda b, t, p: (b, t, 0))
    col = pl.BlockSpec((bb, tt, 1), lambda b, t, p: (b, t, 0))
    gf = jnp.ones((1, D), F32) if g_final is None else g_final.reshape(1, D)
    grid_spec = pltpu.PrefetchScalarGridSpec(
        num_scalar_prefetch=1,
        grid=(B // bb, T // tt),
        in_specs=[big, pl.BlockSpec((bb, 1, D), lambda b, t, p: (b, 0, 0)), col, col,
                  pl.BlockSpec((1, D), lambda b, t, p: (0, 0)),
                  pl.BlockSpec(memory_space=pl.ANY)],
        out_specs=big,
        scratch_shapes=[pltpu.VMEM((2, 2, rows, D), F32), pltpu.SemaphoreType.DMA((2,))],
    )
    return pl.pallas_call(
        functools.partial(_combine_kernel, final=g_final is not None),
        grid_spec=grid_spec,
        out_shape=jax.ShapeDtypeStruct((B, T, D), F32),
        compiler_params=_cparams(("arbitrary", "arbitrary")),
        name="combine",
    )(pos, x, gate, w0, w1, gf, y)


def _moe(x1, gate2, g_norm2, scale2, shift2, wr_hi, wr_lo, br, w_e_gate, w_e_up, w_e_down, layer, g_final):
    B, T, D = x1.shape
    M = B * T
    h2, route = _route(x1, g_norm2, scale2, shift2, wr_hi, wr_lo, br)
    tm = 512 if M >= 4096 else 128
    eid = route[0:2].astype(jnp.int32).reshape(2 * M)
    wts = route[2:4]
    onehot = (eid[:, None] == jnp.arange(N_EXPERTS, dtype=jnp.int32)[None, :]).astype(jnp.int32)
    csum = jnp.cumsum(onehot, axis=0)
    counts = csum[-1]
    rank = jnp.sum((csum - onehot) * onehot, axis=1)
    padded = ((counts + tm - 1) // tm) * tm
    starts = jnp.cumsum(padded) - padded
    pos = starts[eid] + rank
    n_tiles = (2 * M) // tm + N_EXPERTS
    Mpad = n_tiles * tm
    token = jnp.tile(jnp.arange(M, dtype=jnp.int32), 2)
    token_of_pos = jnp.zeros((Mpad,), jnp.int32).at[pos].set(token)
    tile_start = jnp.arange(n_tiles, dtype=jnp.int32) * tm
    ends = starts + padded
    tile_expert = jnp.sum((tile_start[:, None] >= ends[None, :]).astype(jnp.int32), axis=1)
    tile_valid = (tile_expert < N_EXPERTS).astype(jnp.int32)
    last_e = jnp.max(jnp.where(counts > 0, jnp.arange(N_EXPERTS, dtype=jnp.int32), 0))
    tile_expert = jnp.where(tile_valid > 0, tile_expert, last_e).astype(jnp.int32)
    out = _experts(tile_expert, tile_valid, token_of_pos, h2.reshape(M, D), w_e_gate, w_e_up, w_e_down,
                   layer, tm)
    w0 = wts[0].reshape(B, T, 1)
    w1 = wts[1].reshape(B, T, 1)
    return _combine(x1, gate2, out, pos.astype(jnp.int32), w0, w1, g_final)


def _conv4(ext_ref, x, cw_ref, cb_ref, rows):
    ext_ref[8:8 + rows, :] = x
    out = cb_ref[...] + ext_ref[5:5 + rows, :] * cw_ref[0:1, :]
    out = out + ext_ref[6:6 + rows, :] * cw_ref[1:2, :]
    out = out + ext_ref[7:7 + rows, :] * cw_ref[2:3, :]
    return out + x * cw_ref[3:4, :]


def _lru_kernel(lx_ref, lg_ref, buf_ref, h0_ref, cw_ref, cb_ref, wa_ref, ba_ref, wx_ref, bx_ref, lam_ref,
                y_ref, nbuf_ref, hl_ref, ext_scr, a_scr, u_scr, h_scr, hc_scr):
    t = pl.program_id(1)
    Tc = lx_ref.shape[0]

    @pl.when(t == 0)
    def _():
        ext_scr[0:5, :] = jnp.zeros((5, BRANCH_W), F32)
        ext_scr[5:8, :] = buf_ref[...]
        hc_scr[...] = h0_ref[...]

    xc = _conv4(ext_scr, lx_ref[...], cw_ref, cb_ref, Tc)
    tail = ext_scr[Tc + 5:Tc + 8, :]
    ext_scr[5:8, :] = tail
    nbuf_ref[...] = tail

    def blockdiag(w_ref):
        return jnp.concatenate(
            [_dot(xc[:, n * LRU_BW:(n + 1) * LRU_BW], w_ref[n]) for n in range(LRU_BLOCKS)], axis=1)

    r = _sigmoid(blockdiag(wa_ref) + ba_ref[...])
    i = _sigmoid(blockdiag(wx_ref) + bx_ref[...])
    log_a = LRU_C * r * _log_sigmoid(lam_ref[...])
    a_scr[...] = jnp.exp(log_a)
    u_scr[...] = jnp.sqrt(-jnp.tanh(log_a) * (jnp.exp(2.0 * log_a) + 1.0)) * (i * xc)

    def body(s, h):
        h = a_scr[pl.ds(s, 1), :] * h + u_scr[pl.ds(s, 1), :]
        h_scr[pl.ds(s, 1), :] = h
        return h

    h = lax.fori_loop(0, Tc, body, hc_scr[...], unroll=8)
    hc_scr[...] = h
    hl_ref[...] = h
    y_ref[...] = (_gelu_tanh(lg_ref[...]) * h_scr[...]).astype(y_ref.dtype)


def _lru(p_lru, buf, h0, conv_w, conv_b, wa, ba, wx, bx, lam):
    B, T, _ = p_lru.shape
    W = BRANCH_W
    Tc = min(T, 256)
    row = lambda a: a.reshape(1, W)
    const = lambda shape: pl.BlockSpec(shape, lambda b, t: (0,) * len(shape))
    y, nbuf, hl = pl.pallas_call(
        _lru_kernel,
        grid=(B, T // Tc),
        in_specs=[pl.BlockSpec((None, Tc, W), lambda b, t: (b, t, 0)),
                  pl.BlockSpec((None, Tc, W), lambda b, t: (b, t, 1)),
                  pl.BlockSpec((None, CONV_W - 1, W), lambda b, t: (b, 0, 0)),
                  pl.BlockSpec((None, 1, W), lambda b, t: (b, 0, 0)),
                  const((CONV_W, W)), const((1, W)),
                  const((LRU_BLOCKS, LRU_BW, LRU_BW)), const((1, W)),
                  const((LRU_BLOCKS, LRU_BW, LRU_BW)), const((1, W)), const((1, W))],
        out_specs=[pl.BlockSpec((None, Tc, W), lambda b, t: (b, t, 0)),
                   pl.BlockSpec((None, CONV_W - 1, W), lambda b, t: (b, 0, 0)),
                   pl.BlockSpec((None, 1, W), lambda b, t: (b, 0, 0))],
        out_shape=[jax.ShapeDtypeStruct((B, T, W), BF16),
                   jax.ShapeDtypeStruct((B, CONV_W - 1, W), F32),
                   jax.ShapeDtypeStruct((B, 1, W), F32)],
        scratch_shapes=[pltpu.VMEM((Tc + 8, W), F32), pltpu.VMEM((Tc, W), F32), pltpu.VMEM((Tc, W), F32),
                        pltpu.VMEM((Tc, W), F32), pltpu.VMEM((1, W), F32)],
        compiler_params=_cparams(("parallel", "arbitrary")),
        name="rglru",
    )(p_lru, p_lru, buf, h0.reshape(B, 1, W), conv_w, row(conv_b), wa, row(ba), wx, row(bx), row(lam))
    return y, nbuf, hl.reshape(B, W)


def _gla_kernel(q_ref, k_ref, v_ref, g_ref, ga_ref, s0_ref, wup_ref, wupt_ref, b_ref, bt_ref, gn_ref,
                y_ref, s_ref, s_scr):
    t = pl.program_id(1)
    nT = pl.num_programs(1)
    Tv = q_ref.shape[0]
    C = max(Tv, CHUNK)

    @pl.when(t == 0)
    def _():
        s_scr[...] = s0_ref[...]

    q = _pad_rows(q_ref[...], C) * (GLA_DK ** -0.5)
    k = _pad_rows(k_ref[...], C)
    v = _pad_rows(v_ref[...], C)
    ga = _pad_rows(ga_ref[...], C)
    la =_log_sigmoid(_dot_hl(ga, wup_ref[...]) + b_ref[...]) / GLA_NORMALIZER
    lat = _log_sigmoid(_dot_hl_nt(wupt_ref[...], ga) + bt_ref[...]) / GLA_NORMALIZER
    if Tv < C:
        la = jnp.where(lax.broadcasted_iota(jnp.int32, la.shape, 0) < Tv, la, 0.0)
        lat = jnp.where(lax.broadcasted_iota(jnp.int32, lat.shape, 1) < Tv, lat, 0.0)
    tri = jnp.where(_tri(C), 1.0, 0.0).astype(BF16)
    bc = _dot_3x(tri, la)
    b_last = bc[C - 1:C, :]
    b_mid = bc[C // 2:C // 2 + 1, :]
    b_last_col = jnp.sum(lat, axis=1, keepdims=True)
    q_in = q * jnp.exp(bc)
    q_att = q * jnp.exp(bc - b_mid)
    k_att = k * jnp.exp(b_mid - bc)
    k_dec = k * jnp.exp(b_last - bc)
    causal = _tri(C)
    gn = gn_ref[...]
    outs = []
    for h in range(GLA_HEADS):
        ks = slice(h * GLA_DK, (h + 1) * GLA_DK)
        vs = slice(h * GLA_DV, (h + 1) * GLA_DV)
        vh = v[:, vs]
        S = s_scr[h]
        att = jnp.where(causal, _dot_nt(q_att[:, ks], k_att[:, ks]), 0.0)
        o = _dot(att, vh) + _dot(q_in[:, ks], S)
        s_scr[h] = S * jnp.exp(b_last_col[ks, :]) + _dot_tn(k_dec[:, ks], vh)
        o = o * lax.rsqrt(jnp.mean(o * o, axis=-1, keepdims=True) + EPS) * gn
        outs.append(o)
    o = jnp.concatenate(outs, axis=1)[0:Tv]
    y_ref[...] = (o * _silu(g_ref[...])).astype(y_ref.dtype)

    @pl.when(t == nT - 1)
    def _():
        s_ref[...] = s_scr[...]


def _gla(p_gla, p_small, S0, w_up, b_a, g_norm):
    B, T, _ = p_gla.shape
    Tv = min(T, CHUNK)
    const = lambda shape: pl.BlockSpec(shape, lambda b, t: (0,) * len(shape))
    st = (GLA_HEADS, GLA_DK, GLA_DV)
    w_up_pad = jnp.concatenate([w_up, jnp.zeros((128 - GLA_RANK, GLA_QK), F32)], axis=0)
    y, S = pl.pallas_call(
        _gla_kernel,
        grid=(B, T // Tv),
        in_specs=[pl.BlockSpec((None, Tv, GLA_QK), lambda b, t: (b, t, 0)),
                  pl.BlockSpec((None, Tv, GLA_QK), lambda b, t: (b, t, 1)),
                  pl.BlockSpec((None, Tv, BRANCH_W), lambda b, t: (b, t, 1)),
                  pl.BlockSpec((None, Tv, BRANCH_W), lambda b, t: (b, t, 2)),
                  pl.BlockSpec((None, Tv, 128), lambda b, t: (b, t, 0)),
                  pl.BlockSpec((None,) + st, lambda b, t: (b, 0, 0, 0)),
                  const((128, GLA_QK)), const((GLA_QK, 128)),
                  const((1, GLA_QK)), const((GLA_QK, 1)), const((1, GLA_DV))],
        out_specs=[pl.BlockSpec((None, Tv, BRANCH_W), lambda b, t: (b, t, 0)),
                   pl.BlockSpec((None,) + st, lambda b, t: (b, 0, 0, 0))],
        out_shape=[jax.ShapeDtypeStruct((B, T, BRANCH_W), BF16),
                   jax.ShapeDtypeStruct((B,) + st, F32)],
        scratch_shapes=[pltpu.VMEM(st, F32)],
        compiler_params=_cparams(("parallel", "arbitrary")),
        name="gla",
    )(p_gla, p_gla, p_gla, p_gla, p_small, S0, w_up_pad, w_up_pad.T, b_a.reshape(1, GLA_QK),
      b_a.reshape(GLA_QK, 1), g_norm.reshape(1, GLA_DV))
    return y, S


def _mlstm_kernel(qk_ref, v_ref, o_ref, if_ref, ift_ref, buf_ref, c0_ref, n0_ref, m0_ref,
                  cw_ref, cb_ref, bif_ref, bift_ref, ng_ref,
                  y_ref, nbuf_ref, c_ref, n_ref, m_ref, ext_scr, c_scr, n_scr, m_scr):
    t = pl.program_id(1)
    nT = pl.num_programs(1)
    Tv = qk_ref.shape[0]
    C = max(Tv, CHUNK)
    W2 = 2 * BRANCH_W

    @pl.when(t == 0)
    def _():
        ext_scr[0:5, :] = jnp.zeros((5, W2), F32)
        ext_scr[5:8, :] = buf_ref[...]
        c_scr[...] = c0_ref[...]
        n_scr[...] = n0_ref[...]
        m_scr[...] = jnp.zeros(m_scr.shape, F32)
        m_scr[0:1, 0:ML_HEADS] = m0_ref[...]

    qk = _conv4(ext_scr, qk_ref[...], cw_ref, cb_ref, Tv)
    tail = ext_scr[Tv + 5:Tv + 8, :]
    ext_scr[5:8, :] = tail
    nbuf_ref[...] = tail
    qk = _pad_rows(_silu(qk), C)
    q = qk[:, 0:BRANCH_W]
    k = qk[:, BRANCH_W:W2] * (ML_DH ** -0.5)
    v = _pad_rows(v_ref[...], C)

    gates = _pad_rows(if_ref[...], C) + bif_ref[...]
    i_col = gates
    lf_col = _log_sigmoid(gates)
    gates_t = _pad_rows(ift_ref[...] + bift_ref[...], 16)
    if Tv < C:
        gates_t = jnp.concatenate([gates_t, jnp.zeros((16, C - Tv), F32)], axis=1)
    i_row = gates_t
    lf_row = _log_sigmoid(gates_t)
    if Tv < C:
        rmask = lax.broadcasted_iota(jnp.int32, (C, 128), 0) < Tv
        cmask = lax.broadcasted_iota(jnp.int32, (16, C), 1) < Tv
        i_col = jnp.where(rmask, i_col, NEG_BIG)
        lf_col = jnp.where(rmask, lf_col, 0.0)
        i_row = jnp.where(cmask, i_row, NEG_BIG)
        lf_row = jnp.where(cmask, lf_row, 0.0)
    causal = _tri(C)
    tri = jnp.where(causal, 1.0, 0.0).astype(BF16)
    triu = jnp.where(_tri(C, upper=True), 1.0, 0.0).astype(BF16)
    F_col = _dot_3x(tri, lf_col)
    F_row = _dot_x3(lf_row, triu)
    ng = ng_ref[...]
    outs = []
    for h in range(ML_HEADS):
        hs = slice(h * ML_DH, (h + 1) * ML_DH)
        qh, kh, vh = q[:, hs], k[:, hs], v[:, hs]
        Fc = F_col[:, 16 + ML_HEADS + h:17 + ML_HEADS + h]
        ic = i_col[:, 16 + h:17 + h]
        m_prev = m_scr[0:1, h:h + 1]
        Fr = F_row[ML_HEADS + h:ML_HEADS + h + 1, :]
        log_d = jnp.where(causal, Fc - Fr + i_row[h:h + 1, :], NEG_BIG)
        log_inter = Fc + m_prev
        m_t = jnp.maximum(log_inter, jnp.max(log_d, axis=1, keepdims=True))
        s = _dot_nt(qh, kh) * jnp.exp(log_d - m_t)
        inter = jnp.exp(log_inter - m_t)
        Ch = c_scr[h]
        nh = n_scr[h:h + 1, :]
        num = _dot(s, vh) + inter * _dot(qh, Ch)
        den = jnp.sum(s, axis=1, keepdims=True) + inter * jnp.sum(qh * nh, axis=1, keepdims=True)
        hh = num / jnp.maximum(jnp.abs(den), jnp.exp(-m_t))
        m_new = m_t[C - 1:C, :]
        F_last = Fc[C - 1:C, :]
        decay = jnp.exp(F_last + m_prev - m_new)
        w_s = jnp.exp(F_last - Fc + ic - m_new)
        kw = kh * w_s
        c_scr[h] = decay * Ch + _dot_tn(kw, vh)
        n_scr[h:h + 1, :] = decay * nh + jnp.sum(kw, axis=0, keepdims=True)
        m_scr[0:1, h:h + 1] = m_new
        hh = hh * lax.rsqrt(jnp.mean(hh * hh, axis=-1, keepdims=True) + EPS) * ng[:, hs]
        outs.append(hh)
    hcat = jnp.concatenate(outs, axis=1)[0:Tv]
    y_ref[...] = (hcat * _sigmoid(o_ref[...])).astype(y_ref.dtype)

    @pl.when(t == nT - 1)
    def _():
        c_ref[...] = c_scr[...]
        n_ref[...] = n_scr[...]
        m_ref[...] = m_scr[0:1, 0:ML_HEADS]


def _mlstm(p_ml, p_small, buf, C0, n0, m0, conv_w, conv_b, b_if, norm_g):
    B, T, _ = p_ml.shape
    W = BRANCH_W
    Tv = min(T, CHUNK)
    ift = jnp.swapaxes(p_small[:, :, 16:16 + 2 * ML_HEADS], 1, 2)
    const = lambda shape: pl.BlockSpec(shape, lambda b, t: (0,) * len(shape))
    cst = (ML_HEADS, ML_DH, ML_DH)
    y, nbuf, Cn, nn, mn = pl.pallas_call(
        _mlstm_kernel,
        grid=(B, T // Tv),
        in_specs=[pl.BlockSpec((None, Tv, 2 * W), lambda b, t: (b, t, 0)),
                  pl.BlockSpec((None, Tv, W), lambda b, t: (b, t, 2)),
                  pl.BlockSpec((None, Tv, W), lambda b, t: (b, t, 3)),
                  pl.BlockSpec((None, Tv, 128), lambda b, t: (b, t, 0)),
                  pl.BlockSpec((None, 2 * ML_HEADS, Tv), lambda b, t: (b, 0, t)),
                  pl.BlockSpec((None, CONV_W - 1, 2 * W), lambda b, t: (b, 0, 0)),
                  pl.BlockSpec((None,) + cst, lambda b, t: (b, 0, 0, 0)),
                  pl.BlockSpec((None, ML_HEADS, ML_DH), lambda b, t: (b, 0, 0)),
                  pl.BlockSpec((None, 1, ML_HEADS), lambda b, t: (b, 0, 0)),
                  const((CONV_W, 2 * W)), const((1, 2 * W)),
                  const((1, 128)), const((2 * ML_HEADS, 1)), const((1, W))],
        out_specs=[pl.BlockSpec((None, Tv, W), lambda b, t: (b, t, 0)),
                   pl.BlockSpec((None, CONV_W - 1, 2 * W), lambda b, t: (b, 0, 0)),
                   pl.BlockSpec((None,) + cst, lambda b, t: (b, 0, 0, 0)),
                   pl.BlockSpec((None, ML_HEADS, ML_DH), lambda b, t: (b, 0, 0)),
                   pl.BlockSpec((None, 1, ML_HEADS), lambda b, t: (b, 0, 0))],
        out_shape=[jax.ShapeDtypeStruct((B, T, W), BF16),
                   jax.ShapeDtypeStruct((B, CONV_W - 1, 2 * W), F32),
                   jax.ShapeDtypeStruct((B,) + cst, F32),
                   jax.ShapeDtypeStruct((B, ML_HEADS, ML_DH), F32),
                   jax.ShapeDtypeStruct((B, 1, ML_HEADS), F32)],
        scratch_shapes=[pltpu.VMEM((Tv + 8, 2 * W), F32), pltpu.VMEM(cst, F32),
                        pltpu.VMEM((ML_HEADS, ML_DH), F32), pltpu.VMEM((8, 128), F32)],
        compiler_params=_cparams(("parallel", "arbitrary")),
        name="mlstm",
    )(p_ml, p_ml, p_ml, p_small, ift, buf, C0, n0, m0.reshape(B, 1, ML_HEADS),
      conv_w, conv_b.reshape(1, 2 * W),
      jnp.zeros((1, 128), F32).at[0, 16:16 + 2 * ML_HEADS].set(b_if), b_if.reshape(2 * ML_HEADS, 1),
      norm_g.reshape(1, W))
    return y, nbuf, Cn, nn, mn.reshape(B, ML_HEADS)


def _rwkv_kernel(p_ref, sh_ref, s0_ref, mu_ref, w0_ref, wup_ref, a0_ref, aup_ref, gup_ref,
                 kk_ref, ka_ref, rk_ref, lng_ref, lnb_ref,
                 y_ref, shn_ref, s_ref,
                 ext_scr, nkk_scr, w_scr, kka_scr, k2_scr, x_scr, v_scr, g_scr, bon_scr, yv_scr,
                 st_scr, yr_scr):
    t = pl.program_id(1)
    nT = pl.num_programs(1)
    Tb = p_ref.shape[0]
    W = BRANCH_W

    @pl.when(t == 0)
    def _():
        ext_scr[0:7, :] = jnp.zeros((7, RW_NCOLS), F32)
        ext_scr[7:8, :] = sh_ref[...]
        st_scr[...] = s0_ref[...]

    ones_blk = _seg_ones(256, RW_DH)
    p = p_ref[...]
    ext_scr[8:8 + Tb, :] = p
    prev = ext_scr[7:7 + Tb, :]
    last = p[Tb - 1:Tb, :]
    ext_scr[7:8, :] = last
    shn_ref[...] = last
    pm = p + (prev - p) * mu_ref[...]
    r = pm[:, 0:W]
    k = pm[:, W:2 * W]
    v = pm[:, 2 * W:3 * W]
    wa_d = pm[:, 3 * W:3 * W + 128]
    gd = pm[:, 3 * W + 128:3 * W + 256]
    log_w = -RW_DECAY_SCALE * _sigmoid(w0_ref[...] + _dot(jnp.tanh(wa_d), wup_ref[...]))
    a = _sigmoid(a0_ref[...] + _dot(wa_d, aup_ref[...]))
    g_scr[...] = _dot(_sigmoid(gd), gup_ref[...])
    kkr = k * kk_ref[...]
    kk = kkr / jnp.maximum(jnp.sqrt(_segsum3(kkr * kkr, ones_blk)), 1e-12)
    k2 = k * (1.0 + (a - 1.0) * ka_ref[...])
    bon_scr[...] = _segsum3(r * k2 * rk_ref[...], ones_blk) * v
    w = jnp.exp(log_w)
    kka = kk * a
    c1 = _segsum3(kka * r, ones_blk)
    yv_scr[...] = _segsum3(k2 * r, ones_blk) * v
    x_scr[...] = w * r - kk * c1
    nkk_scr[...] = -kk
    w_scr[...] = w
    kka_scr[...] = kka
    k2_scr[...] = k2
    v_scr[...] = v

    BL = 256
    eye_t = (lax.broadcasted_iota(jnp.int32, (RW_DH, BL), 0)
             == lax.broadcasted_iota(jnp.int32, (RW_DH, BL), 1) % RW_DH)
    eye_f = jnp.where(eye_t, 1.0, 0.0)
    eye_b = eye_f.astype(BF16)
    NG = W // BL

    def step(row, c):
        r1 = pl.ds(row, 1)
        S = [st_scr[:, pl.ds(gi * BL, BL)] for gi in range(NG)]
        for half in range(2):
            gis = list(range(half * NG // 2, (half + 1) * NG // 2))
            lss = [pl.ds(gi * BL, BL) for gi in gis]
            lhs = ([(S[gi] * nkk_scr[r1, ls]).astype(BF16) for gi, ls in zip(gis, lss)]
                   + [eye_b * v_scr[r1, ls].astype(BF16) for ls in lss]
                   + [(S[gi] * x_scr[r1, ls]).astype(BF16) for gi, ls in zip(gis, lss)])
            res = jnp.dot(jnp.concatenate(lhs, axis=0), ones_blk, preferred_element_type=F32)
            n = len(gis)
            for j, gi in enumerate(gis):
                ls = lss[j]
                sa = res[j * RW_DH:(j + 1) * RW_DH]
                vc = res[(n + j) * RW_DH:(n + j + 1) * RW_DH]
                yq = res[(2 * n + j) * RW_DH:(2 * n + j + 1) * RW_DH]
                st_scr[:, ls] = S[gi] * w_scr[r1, ls] + sa * kka_scr[r1, ls] + vc * k2_scr[r1, ls]
                yr_scr[r1, ls] = jnp.sum(yq * eye_f, axis=0, keepdims=True)
        return c

    lax.fori_loop(0, Tb, step, 0, unroll=8)

    y = yr_scr[...] + yv_scr[...]
    mean = _segsum3(y, ones_blk) * (1.0 / RW_DH)
    dlt = y - mean
    var = _segsum3(dlt * dlt, ones_blk) * (1.0 / RW_DH)
    yn = dlt * lax.rsqrt(var + RW_LN_EPS) * lng_ref[...] + lnb_ref[...]
    y_ref[...] = ((yn + bon_scr[...]) * g_scr[...]).astype(y_ref.dtype)

    @pl.when(t == nT - 1)
    def _():
        s_ref[...] = st_scr[...]


def _rwkv(p_rw, shift_prev, S0, mu, w0, w_up, a0, a_up, g_up, k_k, k_a, r_k, ln_g, ln_b):
    B, T, _ = p_rw.shape
    W = BRANCH_W
    Tb = min(T, 256)
    s0 = jnp.transpose(S0, (0, 2, 1, 3)).reshape(B, RW_DH, W)
    row = lambda a: a.reshape(1, -1)
    const = lambda shape: pl.BlockSpec(shape, lambda b, t: (0,) * len(shape))
    rows = lambda: pltpu.VMEM((Tb, W), F32)
    y, shn, S = pl.pallas_call(
        _rwkv_kernel,
        grid=(B, T // Tb),
        in_specs=[pl.BlockSpec((None, Tb, RW_NCOLS), lambda b, t: (b, t, 0)),
                  pl.BlockSpec((None, 1, RW_NCOLS), lambda b, t: (b, 0, 0)),
                  pl.BlockSpec((None, RW_DH, W), lambda b, t: (b, 0, 0)),
                  const((1, RW_NCOLS)), const((1, W)), const((128, W)), const((1, W)), const((128, W)),
                  const((128, W)), const((1, W)), const((1, W)), const((1, W)), const((1, W)), const((1, W))],
        out_specs=[pl.BlockSpec((None, Tb, W), lambda b, t: (b, t, 0)),
                   pl.BlockSpec((None, 1, RW_NCOLS), lambda b, t: (b, 0, 0)),
                   pl.BlockSpec((None, RW_DH, W), lambda b, t: (b, 0, 0))],
        out_shape=[jax.ShapeDtypeStruct((B, T, W), BF16),
                   jax.ShapeDtypeStruct((B, 1, RW_NCOLS), F32),
                   jax.ShapeDtypeStruct((B, RW_DH, W), F32)],
        scratch_shapes=[pltpu.VMEM((Tb + 8, RW_NCOLS), F32),
                        rows(), rows(), rows(), rows(), rows(), rows(), rows(), rows(), rows(),
                        pltpu.VMEM((RW_DH, W), F32), rows()],
        compiler_params=_cparams(("parallel", "arbitrary")),
        name="rwkv7",
    )(p_rw, shift_prev.reshape(B, 1, RW_NCOLS), s0, row(mu), row(w0),
      jnp.concatenate([w_up, jnp.zeros_like(w_up)], axis=0), row(a0),
      jnp.concatenate([jnp.zeros_like(a_up), a_up], axis=0), g_up,
      row(k_k), row(k_a), row(r_k), row(ln_g), row(ln_b))
    S = jnp.transpose(S.reshape(B, RW_DH, RW_HEADS, RW_DH), (0, 2, 1, 3))
    return y, shn.reshape(B, RW_NCOLS), S


def _prep_layer(l, w_in, w_mg_down, w_mg_up, b_mg, w_branch, w_out, w_route_g, b_route_g, w_route_e, b_route_e):
    W = BRANCH_W
    wi = w_in[l]
    offs = [0]
    for s in (GLA_QK, GLA_QK, W, W, GLA_RANK, RW_NCOLS, 2 * W, W, W, 2 * ML_HEADS, W, W):
        offs.append(offs[-1] + s)
    seg = lambda i: wi[:, offs[i]:offs[i + 1]]
    w_gla = jnp.concatenate([seg(0), seg(1), seg(2), seg(3)], axis=1).astype(BF16)
    w_rw = seg(5).astype(BF16)
    w_ml = jnp.concatenate([seg(6), seg(7), seg(8)], axis=1).astype(BF16)
    w_lru = jnp.concatenate([seg(10), seg(11)], axis=1).astype(BF16)
    pad = jnp.zeros((D_MODEL, 256 - GLA_RANK - 2 * ML_HEADS), F32)
    w_small = jnp.concatenate([seg(4), seg(9), pad, w_mg_down[l]], axis=1).astype(BF16)
    wmu = jnp.transpose(w_mg_up[l].reshape(MERGE_RANK, 4, D_MODEL), (1, 0, 2)).astype(BF16)
    bmg = b_mg[l].reshape(4, 1, D_MODEL)
    wr = jnp.concatenate([w_route_g[l], w_route_e[l]], axis=1).T
    wr = jnp.concatenate([wr, jnp.zeros((128 - wr.shape[0], D_MODEL), F32)], axis=0)
    wr_hi = wr.astype(BF16)
    wr_lo = (wr - wr_hi.astype(F32)).astype(BF16)
    br = jnp.concatenate([b_route_g[l], b_route_e[l], jnp.zeros((128 - 20,), F32)]).reshape(128, 1)
    return dict(w_gla=w_gla, w_rw=w_rw, w_ml=w_ml, w_lru=w_lru, w_small=w_small, wmu=wmu, bmg=bmg,
                wb=w_branch[l].astype(BF16), w_out=w_out[l].astype(BF16), wr_hi=wr_hi, wr_lo=wr_lo, br=br)


def _run_trunk(x, mod, states, lw, prep, g_final):
    n_layers = mod.shape[0]
    new_states = []
    for l in range(n_layers):
        pw = prep[l]
        st = [s[l] for s in states]
        gla_S, rw_S, rw_shift, ml_C, ml_n, ml_m, ml_conv, lru_h, lru_conv = st
        m = mod[l]
        shift1, scale1, gate1, shift2, scale2, gate2 = [m[:, i:i + 1, :] for i in range(6)]
        h = _modnorm_call(x, lw["g_norm1"][l], scale1, shift1)
        p_gla = _mm(h, pw["w_gla"], 512, name="proj_gla")
        p_rw = _mm(h, pw["w_rw"], 256, name="proj_rw")
        p_ml = _mm(h, pw["w_ml"], 512, name="proj_ml")
        p_lru = _mm(h, pw["w_lru"], 512, name="proj_lru")
        p_small = _mm(h, pw["w_small"], 512, name="proj_small")
        y_a, gla_S = _gla(p_gla, p_small, gla_S, lw["gla_w_up"][l], lw["gla_b"][l], lw["gla_g_norm"][l])
        y_b, rw_shift, rw_S = _rwkv(p_rw, rw_shift, rw_S, lw["rw_mu"][l], lw["rw_w0"][l], lw["rw_w_up"][l],
                                    lw["rw_a0"][l], lw["rw_a_up"][l], lw["rw_g_up"][l], lw["rw_k_k"][l],
                                    lw["rw_k_a"][l], lw["rw_r_k"][l], lw["rw_ln_g"][l], lw["rw_ln_b"][l])
        y_c, ml_conv, ml_C, ml_n, ml_m = _mlstm(p_ml, p_small, ml_conv, ml_C, ml_n, ml_m, lw["ml_conv_w"][l],
                                                lw["ml_conv_b"][l], lw["ml_b_if"][l], lw["ml_norm_g"][l])
        y_d, lru_conv, lru_h = _lru(p_lru, lru_conv, lru_h, lw["lru_conv_w"][l], lw["lru_conv_b"][l],
                                    lw["lru_wa"][l], lw["lru_ba"][l], lw["lru_wx"][l], lw["lru_bx"][l],
                                    lw["lru_lambda"][l])
        merged = _merge(p_small, (y_a, y_b, y_c, y_d), pw["wmu"], pw["bmg"], pw["wb"])
        x1 = _mm(merged, pw["w_out"], 512, res=x, gate=gate1, name="out_proj")
        x = _moe(x1, gate2, lw["g_norm2"][l], scale2, shift2, pw["wr_hi"], pw["wr_lo"], pw["br"],
                 lw["w_e_gate"], lw["w_e_up"], lw["w_e_down"], l,
                 g_final if l == n_layers - 1 else None)
        new_states.append((gla_S, rw_S, rw_shift, ml_C, ml_n, ml_m, ml_conv, lru_h, lru_conv))
    return x, [jnp.stack([st[i] for st in new_states]) for i in range(9)]


def kernel(x_prompt, x_sample, c_prompt, c_sample, state_gla_S, state_rwkv_S, state_rwkv_shift, state_mlstm_C, state_mlstm_n, state_mlstm_m, state_mlstm_conv, state_lru_h, state_lru_conv, w_ada, b_ada, g_norm1, g_norm2, w_in, gla_w_up, gla_b, gla_g_norm, rw_mu, rw_w0, rw_w_up, rw_a0, rw_a_up, rw_g_up, rw_k_k, rw_k_a, rw_r_k, rw_ln_g, rw_ln_b, ml_conv_w, ml_conv_b, ml_b_if, ml_norm_g, lru_conv_w, lru_conv_b, lru_wa, lru_ba, lru_wx, lru_bx, lru_lambda, w_branch, w_mg_down, w_mg_up, b_mg, w_out, w_route_g, b_route_g, w_route_e, b_route_e, w_e_gate, w_e_up, w_e_down, g_final):
    n_layers = w_ada.shape[0]
    Bp, Bs = x_prompt.shape[0], x_sample.shape[0]
    lw = dict(g_norm1=g_norm1, g_norm2=g_norm2, gla_w_up=gla_w_up, gla_b=gla_b, gla_g_norm=gla_g_norm,
              rw_mu=rw_mu, rw_w0=rw_w0, rw_w_up=rw_w_up, rw_a0=rw_a0, rw_a_up=rw_a_up, rw_g_up=rw_g_up,
              rw_k_k=rw_k_k, rw_k_a=rw_k_a, rw_r_k=rw_r_k, rw_ln_g=rw_ln_g, rw_ln_b=rw_ln_b,
              ml_conv_w=ml_conv_w, ml_conv_b=ml_conv_b, ml_b_if=ml_b_if, ml_norm_g=ml_norm_g,
              lru_conv_w=lru_conv_w, lru_conv_b=lru_conv_b, lru_wa=lru_wa, lru_ba=lru_ba, lru_wx=lru_wx,
              lru_bx=lru_bx, lru_lambda=lru_lambda, w_e_gate=w_e_gate, w_e_up=w_e_up, w_e_down=w_e_down)
    prep = [_prep_layer(l, w_in, w_mg_down, w_mg_up, b_mg, w_branch, w_out,
                        w_route_g, b_route_g, w_route_e, b_route_e) for l in range(n_layers)]
    nb = Bp + Bs
    rows = ((nb + 15) // 16) * 16
    c_all = jnp.concatenate([c_prompt, c_sample, jnp.zeros((rows - nb, D_MODEL), F32)], axis=0)
    mod = _ada_mod(c_all, w_ada, b_ada).reshape(n_layers, rows, 6, D_MODEL)
    mod_p = mod[:, 0:Bp]
    mod_s = mod[:, Bp:nb]

    def zeros(*s):
        return jnp.zeros((n_layers, Bp) + s, F32)

    zero_states = (zeros(GLA_HEADS, GLA_DK, GLA_DV), zeros(RW_HEADS, RW_DH, RW_DH), zeros(RW_NCOLS),
                   zeros(ML_HEADS, ML_DH, ML_DH), zeros(ML_HEADS, ML_DH), zeros(ML_HEADS),
                   zeros(CONV_W - 1, 2 * BRANCH_W), zeros(BRANCH_W), zeros(CONV_W - 1, BRANCH_W))
    sample_states = (state_gla_S, state_rwkv_S, state_rwkv_shift, state_mlstm_C, state_mlstm_n,
                     state_mlstm_m, state_mlstm_conv, state_lru_h, state_lru_conv)
    y_prompt, ps = _run_trunk(x_prompt, mod_p, zero_states, lw, prep, g_final)
    y_sample, ss = _run_trunk(x_sample, mod_s, sample_states, lw, prep, g_final)
    return (y_prompt, y_sample, *ps, *ss)
```
